```python
import math
import jax
import jax.numpy as jnp
from jax import lax
import numpy as np

D_MODEL = 1024
BATCH = 8
SEQ = 4096
DEPTH = 4

N_BRANCH = 4
BR_WIDTH = 512
CONV_K = 4
NORM_EPS = 1e-6

GDN_HEADS = 4
GDN_DK = 128
GDN_DV = 128
GDN_CHUNK = 64

GLA_HEADS = 4
GLA_DK = 64
GLA_DV = 128
GLA_LOWRANK = 16
GLA_GATE_NORMALIZER = 16.0
GLA_CHUNK = 64

SSD_HEADS = 8
SSD_HEADDIM = 64
SSD_STATE = 128
SSD_GROUPS = 2
SSD_CHUNK = 64

NSA_HEADS = 8
NSA_KV_HEADS = 2
NSA_HEADDIM = 64
CMP_BLOCK = 32
CMP_STRIDE = 16
SEL_BLOCK = 64
SEL_TOPK = 16
WINDOW = 512
NSA_QBLOCK = 64
FORCED_SCORE = 1e4

GDN_QK = GDN_HEADS * GDN_DK
GDN_V = GDN_HEADS * GDN_DV
GLA_K = GLA_HEADS * GLA_DK
GLA_V = GLA_HEADS * GLA_DV
SSD_INNER = SSD_HEADS * SSD_HEADDIM
SSD_BC = SSD_GROUPS * SSD_STATE
NSA_Q = NSA_HEADS * NSA_HEADDIM
NSA_KV = NSA_KV_HEADS * NSA_HEADDIM

IN_SPLITS = (
    ('gdn_q', GDN_QK), ('gdn_k', GDN_QK), ('gdn_v', GDN_V), ('gdn_beta', GDN_HEADS), ('gdn_a', GDN_HEADS), ('gdn_z', BR_WIDTH),
    ('gla_q', GLA_K), ('gla_k', GLA_K), ('gla_v', GLA_V), ('gla_gk', GLA_LOWRANK), ('gla_z', BR_WIDTH),
    ('ssd_x', SSD_INNER), ('ssd_b', SSD_BC), ('ssd_c', SSD_BC), ('ssd_dt', SSD_HEADS), ('ssd_z', BR_WIDTH),
    ('nsa_q', NSA_Q), ('nsa_kc', NSA_KV), ('nsa_vc', NSA_KV), ('nsa_ks', NSA_KV), ('nsa_vs', NSA_KV),
    ('nsa_kw', NSA_KV), ('nsa_vw', NSA_KV), ('nsa_gate', 3 * NSA_HEADS), ('nsa_z', BR_WIDTH),
    ('merge_gate', N_BRANCH * D_MODEL),
)
D_IN = sum(w for _, w in IN_SPLITS)
F32 = jnp.float32

kernel_name = 'hybrid_gdn_gla_ssd_nsa_block'


def _in_offsets():
    offs, start = {}, 0
    for name, width in IN_SPLITS:
        offs[name] = (start, start + width)
        start += width
    return offs


def _cols(u, offs, first, last):
    return u[..., offs[first][0]:offs[last][1]]


def rmsnorm(x, gain):
    xf = x.astype(F32)
    y = xf * lax.rsqrt(jnp.mean(xf * xf, axis=-1, keepdims=True) + NORM_EPS)
    return (y * gain.astype(F32)).astype(x.dtype)


def l2norm(x):
    return x * lax.rsqrt(jnp.sum(x * x, axis=-1, keepdims=True) + NORM_EPS)


def causal_conv(x, w):
    k_width, ch = w.shape
    return lax.conv_general_dilated(x, w[:, None, :], window_strides=(1,), padding=((k_width - 1, 0),),
                                    dimension_numbers=('NWC', 'WIO', 'NWC'), feature_group_count=ch)


def masked_softmax(s, mask):
    s = jnp.where(mask, s.astype(F32), -jnp.inf)
    m = jnp.max(s, axis=-1, keepdims=True)
    m = jnp.where(jnp.isfinite(m), m, 0.0)
    e = jnp.exp(s - m)
    return e / jnp.maximum(jnp.sum(e, axis=-1, keepdims=True), 1e-30)


def alibi_slopes(n):
    return 2.0 ** (-8.0 * jnp.arange(1, n + 1, dtype=F32) / n)


def gated_delta_rule(q, k, v, beta, g):
    bsz, nh, s_len, dk = q.shape
    dv = v.shape[-1]
    c = GDN_CHUNK
    n = s_len // c
    ch = lambda t: t.reshape(bsz, nh, n, c, *t.shape[3:])
    q = ch(q * dk ** -0.5)
    k, v, beta = ch(k), ch(v), ch(beta)
    gc = jnp.cumsum(ch(g), axis=-1)
    tri = jnp.tril(jnp.ones((c, c), bool))
    stri = jnp.tril(jnp.ones((c, c), bool), -1)
    decay = jnp.exp(jnp.where(tri, gc[..., :, None] - gc[..., None, :], -jnp.inf))
    kb = k * beta[..., None]
    lower = jnp.where(stri, jnp.einsum('bhnid,bhnjd->bhnij', kb, k) * decay, 0.0)
    rhs = jnp.concatenate([v * beta[..., None], kb * jnp.exp(gc)[..., None]], axis=-1)
    sol = lax.linalg.triangular_solve(lower + jnp.eye(c, dtype=F32), rhs, left_side=True, lower=True,
                                      unit_diagonal=True)
    u_c, w_c = sol[..., :dv], sol[..., dv:]
    attn = jnp.einsum('bhnid,bhnjd->bhnij', q, k) * decay
    qg = q * jnp.exp(gc)[..., None]
    kd = k * jnp.exp(gc[..., -1:] - gc)[..., None]
    dec = jnp.exp(gc[..., -1])

    def step(state, inp):
        u_i, w_i, qg_i, kd_i, a_i, d_i = inp
        v_new = u_i - jnp.einsum('bhcd,bhde->bhce', w_i, state)
        o = jnp.einsum('bhcd,bhde->bhce', qg_i, state) + jnp.einsum('bhij,bhje->bhie', a_i, v_new)
        state = state * d_i[..., None, None] + jnp.einsum('bhcd,bhce->bhde', kd_i, v_new)
        return state, o

    xs = tuple(jnp.moveaxis(t, 2, 0) for t in (u_c, w_c, qg, kd, attn, dec))
    _, o = lax.scan(step, jnp.zeros((bsz, nh, dk, dv), F32), xs)
    return jnp.moveaxis(o, 0, 2).reshape(bsz, nh, s_len, dv)


def gdn_branch(u, offs, conv_w, a_log, dt_bias, onorm):
    bsz, s_len, _ = u.shape
    c = lambda name: _cols(u, offs, name, name).astype(F32)
    qkv = jax.nn.silu(causal_conv(_cols(u, offs, 'gdn_q', 'gdn_v').astype(F32), conv_w.astype(F32)))
    q, k, v = jnp.split(qkv, [GDN_QK, 2 * GDN_QK], axis=-1)
    heads = lambda t, d: t.reshape(bsz, s_len, GDN_HEADS, d).transpose(0, 2, 1, 3)
    q, k, v = l2norm(heads(q, GDN_DK)), l2norm(heads(k, GDN_DK)), heads(v, GDN_DV)
    beta = jax.nn.sigmoid(c('gdn_beta')).transpose(0, 2, 1)
    g = (-jnp.exp(a_log.astype(F32)) * jax.nn.softplus(c('gdn_a') + dt_bias.astype(F32))).transpose(0, 2, 1)
    o = gated_delta_rule(q, k, v, beta, g)
    o = rmsnorm(o, onorm).transpose(0, 2, 1, 3).reshape(bsz, s_len, GDN_V)
    return o * jax.nn.silu(c('gdn_z'))


def gla_chunked(q, k, v, gk):
    bsz, nh, s_len, dk = q.shape
    dv = v.shape[-1]
    c = GLA_CHUNK
    n = s_len // c
    ch = lambda t: t.reshape(bsz, nh, n, c, t.shape[-1])
    q = ch(q * dk ** -0.5)
    k, v = ch(k), ch(v)
    b = jnp.cumsum(ch(gk), axis=3)
    bref = b[:, :, :, c // 2:c // 2 + 1]
    tri = jnp.tril(jnp.ones((c, c), bool))
    a_intra = jnp.where(tri, jnp.einsum('bhnid,bhnjd->bhnij', q * jnp.exp(b - bref), k * jnp.exp(bref - b)), 0.0)
    o_intra = jnp.einsum('bhnij,bhnje->bhnie', a_intra, v)
    qg = q * jnp.exp(b)
    kd = k * jnp.exp(b[:, :, :, -1:] - b)
    dec = jnp.exp(b[:, :, :, -1])

    def step(state, inp):
        qg_i, kd_i, v_i, d_i = inp
        o = jnp.einsum('bhcd,bhde->bhce', qg_i, state)
        state = state * d_i[..., None] + jnp.einsum('bhcd,bhce->bhde', kd_i, v_i)
        return state, o

    xs = tuple(jnp.moveaxis(t, 2, 0) for t in (qg, kd, v, dec))
    _, o_inter = lax.scan(step, jnp.zeros((bsz, nh, dk, dv), F32), xs)
    return (o_intra + jnp.moveaxis(o_inter, 0, 2)).reshape(bsz, nh, s_len, dv)


def gla_branch(u, offs, w_gk, b_gk, onorm):
    bsz, s_len, _ = u.shape
    c = lambda name: _cols(u, offs, name, name).astype(F32)
    gk = jax.nn.log_sigmoid(c('gla_gk') @ w_gk.astype(F32) + b_gk.astype(F32)) / GLA_GATE_NORMALIZER
    heads = lambda t, d: t.reshape(bsz, s_len, GLA_HEADS, d).transpose(0, 2, 1, 3)
    o = gla_chunked(heads(c('gla_q'), GLA_DK), heads(c('gla_k'), GLA_DK), heads(c('gla_v'), GLA_DV), heads(gk, GLA_DK))
    o = rmsnorm(o, onorm).transpose(0, 2, 1, 3).reshape(bsz, s_len, GLA_V)
    return o * jax.nn.silu(c('gla_z'))


def ssd_chunked(x, dt, a_head, bm, cm, d_skip):
    bsz, s_len, ng, hg, p = x.shape
    ns = bm.shape[-1]
    c = SSD_CHUNK
    n = s_len // c
    xc = x.reshape(bsz, n, c, ng, hg, p)
    dtc = dt.reshape(bsz, n, c, ng, hg)
    bc = bm.reshape(bsz, n, c, ng, ns)
    cc = cm.reshape(bsz, n, c, ng, ns)
    acs = jnp.cumsum(dtc * a_head, axis=2)
    xdt = xc * dtc[..., None]
    acs_h = jnp.moveaxis(acs, 2, -1)
    tri = jnp.tril(jnp.ones((c, c), bool))
    lmat = jnp.exp(jnp.where(tri, acs_h[..., :, None] - acs_h[..., None, :], -jnp.inf))
    cb = jnp.einsum('bnigs,bnjgs->bngij', cc, bc)
    y_diag = jnp.einsum('bngij,bnghij,bnjghp->bnighp', cb, lmat, xdt)
    decay_states = jnp.exp(acs[:, :, -1:] - acs)
    states = jnp.einsum('bnjgs,bnjgh,bnjghp->bnghps', bc, decay_states, xdt)
    chunk_decay = jnp.exp(acs[:, :, -1])

    def step(h, inp):
        st, dc = inp
        return h * dc[..., None, None] + st, h

    _, h_prev = lax.scan(step, jnp.zeros((bsz, ng, hg, p, ns), F32),
                         (jnp.moveaxis(states, 1, 0), jnp.moveaxis(chunk_decay, 1, 0)))
    h_prev = jnp.moveaxis(h_prev, 0, 1)
    y_off = jnp.einsum('bnigs,bnghps,bnigh->bnighp', cc, h_prev, jnp.exp(acs))
    y = y_diag + y_off + xc * d_skip[..., None]
    return y.reshape(bsz, s_len, ng, hg, p)


def ssd_branch(u, offs, conv_w, conv_b, a_log, dt_bias, d_skip, onorm):
    bsz, s_len, _ = u.shape
    hg = SSD_HEADS // SSD_GROUPS
    c = lambda name: _cols(u, offs, name, name).astype(F32)
    xbc = jax.nn.silu(causal_conv(_cols(u, offs, 'ssd_x', 'ssd_c').astype(F32), conv_w.astype(F32)) + conv_b.astype(F32))
    xs, bm, cm = jnp.split(xbc, [SSD_INNER, SSD_INNER + SSD_BC], axis=-1)
    xs = xs.reshape(bsz, s_len, SSD_GROUPS, hg, SSD_HEADDIM)
    bm = bm.reshape(bsz, s_len, SSD_GROUPS, SSD_STATE)
    cm = cm.reshape(bsz, s_len, SSD_GROUPS, SSD_STATE)
    dt = jax.nn.softplus(c('ssd_dt') + dt_bias.astype(F32)).reshape(bsz, s_len, SSD_GROUPS, hg)
    a_head = -jnp.exp(a_log.astype(F32)).reshape(SSD_GROUPS, hg)
    y = ssd_chunked(xs, dt, a_head, bm, cm, d_skip.astype(F32).reshape(SSD_GROUPS, hg)).reshape(bsz, s_len, SSD_INNER)
    return rmsnorm(y * jax.nn.silu(c('ssd_z')), onorm)


def compress_tokens(kv, pos_emb, w1, w2):
    bsz, ng, s_len, hd = kv.shape
    r = CMP_BLOCK // CMP_STRIDE
    n_sub = s_len // CMP_STRIDE
    nc = n_sub - r + 1
    sub = kv.reshape(bsz, ng, n_sub, CMP_STRIDE, hd)
    blocks = jnp.concatenate([sub[:, :, i:i + nc] for i in range(r)], axis=3) + pos_emb.astype(F32)
    flat = blocks.reshape(bsz, ng, nc, CMP_BLOCK * hd)
    return jax.nn.silu(flat @ w1.astype(F32)) @ w2.astype(F32)


def nsa_attention(q, kc, vc, ks, vs, kw, vw, gates):
    bsz, ng, hg, s_len, hd = q.shape
    nc = kc.shape[2]
    nsel = s_len // SEL_BLOCK
    topk = min(SEL_TOPK, nsel)
    scale = hd ** -0.5
    slopes = alibi_slopes(ng * hg).reshape(ng, hg, 1, 1)
    cmp_start = jnp.arange(nc) * CMP_STRIDE
    cmp_pos = cmp_start + CMP_BLOCK - 1
    sel_start = jnp.arange(nsel) * SEL_BLOCK
    overlap = ((cmp_start[:, None] <= sel_start[None, :] + SEL_BLOCK - 1)
               & (cmp_pos[:, None] >= sel_start[None, :])).astype(F32)
    ks_blk = ks.reshape(bsz, ng, nsel, SEL_BLOCK, hd)
    vs_blk = vs.reshape(bsz, ng, nsel, SEL_BLOCK, hd)
    kw_pad = jnp.pad(kw, ((0, 0), (0, 0), (WINDOW, 0), (0, 0)))
    vw_pad = jnp.pad(vw, ((0, 0), (0, 0), (WINDOW, 0), (0, 0)))
    bi = jnp.arange(bsz)[:, None, None, None]
    gi = jnp.arange(ng)[None, :, None, None]
    jj = jnp.arange(nsel)
    in_blk = jnp.arange(SEL_BLOCK)

    def block(q0):
        qb = lax.dynamic_slice_in_dim(q, q0, NSA_QBLOCK, axis=3)
        t = q0 + jnp.arange(NSA_QBLOCK)
        s = jnp.einsum('bghqd,bgkd->bghqk', qb, kc) * scale - slopes * (t[:, None] - cmp_pos[None, :]).astype(F32)
        p_c = masked_softmax(s, cmp_pos[None, :] <= t[:, None])
        o_c = jnp.einsum('bghqk,bgkd->bghqd', p_c, vc)
        imp = jnp.einsum('bgqk,kj->bgqj', jnp.sum(p_c, axis=2), overlap)
        cur = t // SEL_BLOCK
        blk_valid = jj[None, :] <= cur[:, None]
        forced = (jj[None, :] == 0) | (jj[None, :] == cur[:, None]) | (jj[None, :] == cur[:, None] - 1)
        imp = jnp.where(blk_valid, jnp.where(forced, FORCED_SCORE, imp), -1.0)
        vals, idx = lax.top_k(imp, topk)
        pos_b = idx[..., None] * SEL_BLOCK + in_blk
        ok = ((vals >= 0.0)[..., None] & (pos_b <= t[:, None, None])).reshape(bsz, ng, NSA_QBLOCK, topk * SEL_BLOCK)
        pos_s = pos_b.reshape(bsz, ng, NSA_QBLOCK, topk * SEL_BLOCK)
        k_g = ks_blk[bi, gi, idx].reshape(bsz, ng, NSA_QBLOCK, topk * SEL_BLOCK, hd)
        v_g = vs_blk[bi, gi, idx].reshape(bsz, ng, NSA_QBLOCK, topk * SEL_BLOCK, hd)
        s = jnp.einsum('bghqd,bgqkd->bghqk', qb, k_g) * scale - slopes * (t[:, None] - pos_s[:, :, None]).astype(F32)
        p_s = masked_softmax(s, ok[:, :, None])
        o_s = jnp.einsum('bghqk,bgqkd->bghqd', p_s, v_g)
        kwb = lax.dynamic_slice_in_dim(kw_pad, q0, WINDOW + NSA_QBLOCK, axis=2)
        vwb = lax.dynamic_slice_in_dim(vw_pad, q0, WINDOW + NSA_QBLOCK, axis=2)
        pos_w = q0 - WINDOW + jnp.arange(WINDOW + NSA_QBLOCK)
        dist = t[:, None] - pos_w[None, :]
        okw = (dist >= 0) & (dist < WINDOW) & (pos_w[None, :] >= 0)
        s = jnp.einsum('bghqd,bgkd->bghqk', qb, kwb) * scale - slopes * dist.astype(F32)
        o_w = jnp.einsum('bghqk,bgkd->bghqd', masked_softmax(s, okw), vwb)
        gb = lax.dynamic_slice_in_dim(gates, q0, NSA_QBLOCK, axis=3)
        return gb[..., 0:1] * o_c + gb[..., 1:2] * o_s + gb[..., 2:3] * o_w

    starts = jnp.arange(s_len // NSA_QBLOCK, dtype=jnp.int32) * NSA_QBLOCK
    out = lax.map(block, starts)
    return jnp.moveaxis(out, 0, 3).reshape(bsz, ng, hg, s_len, hd)


def nsa_branch(u, offs, cmp_pos_k, cmp_pos_v, w_ck1, w_ck2, w_cv1, w_cv2):
    bsz, s_len, _ = u.shape
    ng, hg, hd = NSA_KV_HEADS, NSA_HEADS // NSA_KV_HEADS, NSA_HEADDIM
    c = lambda name: _cols(u, offs, name, name).astype(F32)
    q = c('nsa_q').reshape(bsz, s_len, ng, hg, hd).transpose(0, 2, 3, 1, 4)
    kvh = lambda t: t.reshape(bsz, s_len, ng, hd).transpose(0, 2, 1, 3)
    kc = compress_tokens(kvh(c('nsa_kc')), cmp_pos_k, w_ck1, w_ck2)
    vc = compress_tokens(kvh(c('nsa_vc')), cmp_pos_v, w_cv1, w_cv2)
    gates = jax.nn.sigmoid(c('nsa_gate')).reshape(bsz, s_len, ng, hg, 3).transpose(0, 2, 3, 1, 4)
    o = nsa_attention(q, kc, vc, kvh(c('nsa_ks')), kvh(c('nsa_vs')), kvh(c('nsa_kw')), kvh(c('nsa_vw')), gates)
    o = o.transpose(0, 3, 1, 2, 4).reshape(bsz, s_len, NSA_Q)
    return o * jax.nn.silu(c('nsa_z'))


def _dt_bias(key, n):
    dt = jnp.exp(jax.random.uniform(key, (DEPTH, n), F32, minval=math.log(1e-3), maxval=math.log(1e-1)))
    return dt + jnp.log(-jnp.expm1(-dt))


def setup_inputs(seed: int = 0) -> dict:
    key = jax.random.key(seed)
    ks = jax.random.split(key, 32)
    nrm = lambda k, shape, sc: jax.random.normal(k, shape, F32) * sc
    gain = lambda k, n: 1.0 + 0.02 * jax.random.normal(k, (DEPTH, n), F32)
    a_log = lambda k, n: jnp.log(jax.random.uniform(k, (DEPTH, n), F32, minval=1.0, maxval=16.0))
    hd = NSA_HEADDIM
    return {
        'x': jax.random.normal(ks[0], (BATCH, SEQ, D_MODEL), F32),
        'norm_pre': gain(ks[1], D_MODEL),
        'norm_post': gain(ks[2], D_MODEL),
        'w_in': nrm(ks[3], (DEPTH, D_MODEL, D_IN), D_MODEL ** -0.5),
        'conv_a': nrm(ks[4], (DEPTH, CONV_K, 2 * GDN_QK + GDN_V), CONV_K ** -0.5),
        'a_log_a': a_log(ks[5], GDN_HEADS),
        'dt_bias_a': _dt_bias(ks[6], GDN_HEADS),
        'onorm_a': gain(ks[7], GDN_DV),
        'w_gk': nrm(ks[8], (DEPTH, GLA_LOWRANK, GLA_K), GLA_LOWRANK ** -0.5),
        'b_gk': nrm(ks[9], (DEPTH, GLA_K), 0.02),
        'onorm_b': gain(ks[10], GLA_DV),
        'conv_c': nrm(ks[11], (DEPTH, CONV_K, SSD_INNER + 2 * SSD_BC), CONV_K ** -0.5),
        'conv_bias_c': nrm(ks[12], (DEPTH, SSD_INNER + 2 * SSD_BC), 0.02),
        'a_log_c': a_log(ks[13], SSD_HEADS),
        'dt_bias_c': _dt_bias(ks[14], SSD_HEADS),
        'd_skip_c': 1.0 + 0.1 * jax.random.normal(ks[15], (DEPTH, SSD_HEADS), F32),
        'onorm_c': gain(ks[16], SSD_INNER),
        'cmp_pos_k': nrm(ks[17], (DEPTH, CMP_BLOCK, hd), 0.02),
        'cmp_pos_v': nrm(ks[18], (DEPTH, CMP_BLOCK, hd), 0.02),
        'w_ck1': nrm(ks[19], (DEPTH, CMP_BLOCK * hd, hd), (CMP_BLOCK * hd) ** -0.5),
        'w_ck2': nrm(ks[20], (DEPTH, hd, hd), hd ** -0.5),
        'w_cv1': nrm(ks[21], (DEPTH, CMP_BLOCK * hd, hd), (CMP_BLOCK * hd) ** -0.5),
        'w_cv2': nrm(ks[22], (DEPTH, hd, hd), hd ** -0.5),
        'w_br': nrm(ks[23], (DEPTH, N_BRANCH, BR_WIDTH, D_MODEL), BR_WIDTH ** -0.5),
        'w_out': nrm(ks[24], (DEPTH, D_MODEL, D_MODEL), D_MODEL ** -0.5),
    }


def reference(x, norm_pre, norm_post, w_in, conv_a, a_log_a, dt_bias_a, onorm_a, w_gk, b_gk, onorm_b,
              conv_c, conv_bias_c, a_log_c, dt_bias_c, d_skip_c, onorm_c, cmp_pos_k, cmp_pos_v,
              w_ck1, w_ck2, w_cv1, w_cv2, w_br, w_out):
    offs = _in_offsets()
    bsz, s_len, _ = x.shape
    for l in range(DEPTH):
        h = rmsnorm(x, norm_pre[l])
        u = h @ w_in[l]
        branches = (
            gdn_branch(u, offs, conv_a[l], a_log_a[l], dt_bias_a[l], onorm_a[l]),
            gla_branch(u, offs, w_gk[l], b_gk[l], onorm_b[l]),
            ssd_branch(u, offs, conv_c[l], conv_bias_c[l], a_log_c[l], dt_bias_c[l], d_skip_c[l], onorm_c[l]),
            nsa_branch(u, offs, cmp_pos_k[l], cmp_pos_v[l], w_ck1[l], w_ck2[l], w_cv1[l], w_cv2[l]),
        )
        gates = jax.nn.sigmoid(_cols(u, offs, 'merge_gate', 'merge_gate').astype(F32)).reshape(bsz, s_len, N_BRANCH, D_MODEL)
        merged = jnp.zeros((bsz, s_len, D_MODEL), F32)
        for n, y in enumerate(branches):
            merged = merged + gates[:, :, n] * (y @ w_br[l, n].astype(F32))
        out = merged.astype(x.dtype) @ w_out[l]
        x = x + rmsnorm(out, norm_post[l])
    return x
```

```python
import functools

import jax
import jax.numpy as jnp
from jax import lax
from jax.experimental import pallas as pl
from jax.experimental.pallas import tpu as pltpu

D_MODEL = 1024
N_BRANCH = 4
BR_WIDTH = 512
CONV_K = 4
NORM_EPS = 1e-6

GDN_HEADS = 4
GDN_DK = 128
GDN_DV = 128
GLA_HEADS = 4
GLA_DK = 64
GLA_DV = 128
GLA_LOWRANK = 16
GLA_GATE_NORMALIZER = 16.0
SSD_HEADS = 8
SSD_HEADDIM = 64
SSD_STATE = 128
SSD_GROUPS = 2
NSA_HEADS = 8
NSA_KV_HEADS = 2
NSA_HEADDIM = 64
CMP_BLOCK = 32
CMP_STRIDE = 16
SEL_BLOCK = 64
SEL_TOPK = 16
WINDOW = 512
FORCED_SCORE = 1e4
CHUNK = 64

F32 = jnp.float32
BF16 = jnp.bfloat16
_MXU = jnp.bfloat16
_HI = lax.Precision.HIGHEST
_VMEM_LIMIT = 56 * 1024 * 1024
_MASKED = -1e30
_UNSELECTED = -30000.0

_U_LAYOUT = {}
_off = 0
for _name, _w in (
    ('gdn_q', 512), ('gdn_k', 512), ('gdn_v', 512), ('gdn_z', 512),
    ('gla_q', 256), ('gla_k', 256), ('gla_v', 512), ('gla_z', 512),
    ('ssd_x', 512), ('ssd_b', 256), ('ssd_c', 256), ('ssd_z', 512),
    ('nsa_q', 512), ('nsa_kc', 128), ('nsa_vc', 128), ('nsa_ks', 128), ('nsa_vs', 128),
    ('nsa_kw', 128), ('nsa_vw', 128), ('nsa_z', 512),
    ('small', 128), ('pad', 128), ('merge_gate', 4096),
):
    _U_LAYOUT[_name] = (_off, _w)
    _off += _w
U_WIDTH = _off
SM_BETA, SM_A, SM_GK, SM_DT, SM_GATE = 0, 4, 8, 24, 32

_IN_SPLITS = (
    ('gdn_q', 512), ('gdn_k', 512), ('gdn_v', 512), ('gdn_beta', 4), ('gdn_a', 4), ('gdn_z', 512),
    ('gla_q', 256), ('gla_k', 256), ('gla_v', 512), ('gla_gk', 16), ('gla_z', 512),
    ('ssd_x', 512), ('ssd_b', 256), ('ssd_c', 256), ('ssd_dt', 8), ('ssd_z', 512),
    ('nsa_q', 512), ('nsa_kc', 128), ('nsa_vc', 128), ('nsa_ks', 128), ('nsa_vs', 128),
    ('nsa_kw', 128), ('nsa_vw', 128), ('nsa_gate', 24), ('nsa_z', 512),
    ('merge_gate', 4096),
)
_SMALL_COL = {'gdn_beta': SM_BETA, 'gdn_a': SM_A, 'gla_gk': SM_GK, 'ssd_dt': SM_DT, 'nsa_gate': SM_GATE}


def _layout_weight(w_in_l):
    out = jnp.zeros((w_in_l.shape[0], U_WIDTH), w_in_l.dtype)
    start = 0
    for name, width in _IN_SPLITS:
        if name in _SMALL_COL:
            dst = _U_LAYOUT['small'][0] + _SMALL_COL[name]
        else:
            dst = _U_LAYOUT[name][0]
        out = lax.dynamic_update_slice(out, w_in_l[:, start:start + width], (0, dst))
        start += width
    return out


def _mm(a, b):
    return jnp.dot(a.astype(_MXU), b.astype(_MXU), preferred_element_type=F32)


def _mm_nt(a, b):
    return lax.dot_general(a.astype(_MXU), b.astype(_MXU), (((1,), (1,)), ((), ())),
                           preferred_element_type=F32)


def _mm_tn(a, b):
    return lax.dot_general(a.astype(_MXU), b.astype(_MXU), (((0,), (0,)), ((), ())),
                           preferred_element_type=F32)


def _mm_hi(a, b):
    return jnp.dot(a, b, preferred_element_type=F32, precision=_HI)


def _silu(x):
    return x * jax.nn.sigmoid(x)


def _softplus(x):
    return jnp.maximum(x, 0.0) + jnp.log(1.0 + jnp.exp(-jnp.abs(x)))


def _iota2(shape, axis):
    return lax.broadcasted_iota(jnp.int32, shape, axis)


def _causal_conv_silu(xbuf, r0, col, width, w_ref, bias=None):
    xx = xbuf[pl.ds(r0, CHUNK + 8), col:col + width]
    w = w_ref[:, col:col + width]
    acc = xx * w[CONV_K - 1:CONV_K, :]
    for k in range(1, CONV_K):
        acc = acc + pltpu.roll(xx, k, 0) * w[CONV_K - 1 - k:CONV_K - k, :]
    y = acc[8:, :]
    if bias is not None:
        y = y + bias
    return _silu(y)


def _inproj_kernel(x_ref, g_ref, w_ref, o_ref, h_ref):
    @pl.when(pl.program_id(1) == 0)
    def _():
        x = x_ref[...]
        ms = jnp.mean(x * x, axis=-1, keepdims=True)
        h_ref[...] = (x * lax.rsqrt(ms + NORM_EPS) * g_ref[...]).astype(h_ref.dtype)

    o_ref[...] = jnp.dot(h_ref[...], w_ref[...], preferred_element_type=F32)


def _inproj(x2d, gain, w_all, *, tm=1024, tn=1408):
    t = x2d.shape[0]
    tm = min(tm, t)
    return pl.pallas_call(
        _inproj_kernel,
        grid=(t // tm, U_WIDTH // tn),
        in_specs=[
            pl.BlockSpec((tm, D_MODEL), lambda i, j: (i, 0)),
            pl.BlockSpec((1, D_MODEL), lambda i, j: (0, 0)),
            pl.BlockSpec((D_MODEL, tn), lambda i, j: (0, j)),
        ],
        out_specs=pl.BlockSpec((tm, tn), lambda i, j: (i, j)),
        out_shape=jax.ShapeDtypeStruct((t, U_WIDTH), F32),
        scratch_shapes=[pltpu.VMEM((tm, D_MODEL), _MXU)],
        compiler_params=pltpu.CompilerParams(
            dimension_semantics=("arbitrary", "arbitrary"), vmem_limit_bytes=_VMEM_LIMIT),
        name="inproj",
    )(x2d, gain.reshape(1, D_MODEL), w_all)


def _gdn_kernel(qkv_ref, z_ref, sm_ref, cw_ref, alog_ref, dtb_ref, onorm_ref, y_ref, xbuf, state, *, tc):
    c = CHUNK

    @pl.when(pl.program_id(1) == 0)
    def _():
        state[...] = jnp.zeros_like(state)
        xbuf[0:8, :] = jnp.zeros((8, xbuf.shape[1]), F32)

    xbuf[8:8 + tc, :] = qkv_ref[...]

    ri, ci = _iota2((c, c), 0), _iota2((c, c), 1)
    tril = ri >= ci
    stril = ri > ci
    tril_f = tril.astype(F32)
    triu_f = (ri <= ci).astype(F32)
    eye_f = (ri == ci).astype(F32)
    ones_f = jnp.ones((c, c), F32)
    neg_a = -jnp.exp(alog_ref[...])
    dtb = dtb_ref[...]
    onorm = onorm_ref[...]

    def chunk(ck, carry):
        r0 = pl.multiple_of(ck * c, c)
        sm = sm_ref[pl.ds(r0, c), :]
        beta_all = jax.nn.sigmoid(sm)
        g_all = neg_a * _softplus(sm + dtb)
        for h in range(GDN_HEADS):
            q = _causal_conv_silu(xbuf, r0, h * GDN_DK, GDN_DK, cw_ref)
            k = _causal_conv_silu(xbuf, r0, 512 + h * GDN_DK, GDN_DK, cw_ref)
            v = _causal_conv_silu(xbuf, r0, 1024 + h * GDN_DV, GDN_DV, cw_ref)
            q = q * lax.rsqrt(jnp.sum(q * q, axis=-1, keepdims=True) + NORM_EPS) * (GDN_DK ** -0.5)
            k = k * lax.rsqrt(jnp.sum(k * k, axis=-1, keepdims=True) + NORM_EPS)
            beta = jnp.broadcast_to(beta_all[:, SM_BETA + h:SM_BETA + h + 1], (c, GDN_DK))
            gb = jnp.broadcast_to(g_all[:, SM_A + h:SM_A + h + 1], (c, GDN_DK))
            gc = _mm_hi(tril_f, gb)
            gc_row = _mm_hi(ones_f, gb[:, :c] * triu_f)
            decay = jnp.exp(jnp.where(tril, gc[:, :c] - gc_row, -jnp.inf))
            kb = k * beta
            lower = jnp.where(stril, _mm_nt(kb, k) * decay, 0.0)
            npow = -lower
            inv = eye_f + npow
            for _ in range(5):
                npow = _mm_hi(npow, npow)
                inv = inv + _mm_hi(inv, npow)
            egc = jnp.exp(gc)
            u_c = _mm_hi(inv, v * beta)
            w_c = _mm_hi(inv, kb * egc)
            attn = _mm_nt(q, k) * decay
            qg = q * egc
            gc_last = gc[c - 1:c, :]
            kd = k * jnp.exp(gc_last - gc)
            st = state[h]
            v_new = u_c - _mm(w_c, st)
            o = _mm(qg, st) + _mm(attn, v_new)
            state[h] = st * jnp.exp(gc_last) + _mm_tn(kd, v_new)
            o = o * lax.rsqrt(jnp.mean(o * o, axis=-1, keepdims=True) + NORM_EPS) * onorm
            zh = z_ref[pl.ds(r0, c), h * GDN_DV:(h + 1) * GDN_DV]
            y_ref[pl.ds(r0, c), h * GDN_DV:(h + 1) * GDN_DV] = o * _silu(zh)
        return carry

    lax.fori_loop(0, tc // c, chunk, 0)
    xbuf[0:8, :] = qkv_ref[tc - 8:tc, :]


def _small_row(vec, col):
    return jnp.zeros((1, 128), F32).at[0, col:col + vec.shape[0]].set(vec.astype(F32))


def _gdn(u_all, bsz, s_len, conv_w, a_log, dt_bias, onorm, *, tc=256):
    tc = min(tc, s_len)
    ns = s_len // tc
    sm_blk = _U_LAYOUT['small'][0] // 128
    return pl.pallas_call(
        functools.partial(_gdn_kernel, tc=tc),
        grid=(bsz, ns),
        in_specs=[
            pl.BlockSpec((tc, 1536), lambda b, s: (b * ns + s, 0)),
            pl.BlockSpec((tc, 512), lambda b, s: (b * ns + s, _U_LAYOUT['gdn_z'][0] // 512)),
            pl.BlockSpec((tc, 128), lambda b, s: (b * ns + s, sm_blk)),
            pl.BlockSpec((CONV_K, 1536), lambda b, s: (0, 0)),
            pl.BlockSpec((1, 128), lambda b, s: (0, 0)),
            pl.BlockSpec((1, 128), lambda b, s: (0, 0)),
            pl.BlockSpec((1, GDN_DV), lambda b, s: (0, 0)),
        ],
        out_specs=pl.BlockSpec((tc, 512), lambda b, s: (b * ns + s, 0)),
        out_shape=jax.ShapeDtypeStruct((bsz * s_len, 512), F32),
        scratch_shapes=[pltpu.VMEM((tc + 8, 1536), F32), pltpu.VMEM((GDN_HEADS, GDN_DK, GDN_DV), F32)],
        compiler_params=pltpu.CompilerParams(
            dimension_semantics=("arbitrary", "arbitrary"), vmem_limit_bytes=_VMEM_LIMIT),
        name="gdn",
    )(u_all, u_all, u_all, conv_w.astype(F32), _small_row(a_log, SM_A), _small_row(dt_bias, SM_A),
      onorm.reshape(1, GDN_DV).astype(F32))


def _gla_kernel(qk_ref, v_ref, z_ref, sm_ref, wgk_ref, bgk_ref, onorm_ref, y_ref, state_t, *, tc):
    c = CHUNK

    @pl.when(pl.program_id(1) == 0)
    def _():
        state_t[...] = jnp.zeros_like(state_t)

    ri, ci = _iota2((c, c), 0), _iota2((c, c), 1)
    tril = ri >= ci
    tril_f = tril.astype(F32)
    onorm = onorm_ref[...]

    def chunk(ck, carry):
        r0 = pl.multiple_of(ck * c, c)
        pre = _mm(sm_ref[pl.ds(r0, c), :], wgk_ref[...]) + bgk_ref[...]
        gk_all = (jnp.minimum(pre, 0.0) - jnp.log(1.0 + jnp.exp(-jnp.abs(pre)))) * (1.0 / GLA_GATE_NORMALIZER)
        for h in range(GLA_HEADS):
            q = qk_ref[pl.ds(r0, c), h * GLA_DK:(h + 1) * GLA_DK] * (GLA_DK ** -0.5)
            k = qk_ref[pl.ds(r0, c), 256 + h * GLA_DK:256 + (h + 1) * GLA_DK]
            v = v_ref[pl.ds(r0, c), h * GLA_DV:(h + 1) * GLA_DV]
            b = _mm_hi(tril_f, gk_all[:, h * GLA_DK:(h + 1) * GLA_DK])
            bref = b[c // 2:c // 2 + 1, :]
            b_last = b[c - 1:c, :]
            a_intra = jnp.where(tril, _mm_nt(q * jnp.exp(b - bref), k * jnp.exp(bref - b)), 0.0)
            st = state_t[h]
            o = _mm(a_intra, v) + _mm_nt(q * jnp.exp(b), st)
            state_t[h] = st * jnp.exp(b_last) + _mm_tn(v, k * jnp.exp(b_last - b))
            o = o * lax.rsqrt(jnp.mean(o * o, axis=-1, keepdims=True) + NORM_EPS) * onorm
            zh = z_ref[pl.ds(r0, c), h * GLA_DV:(h + 1) * GLA_DV]
            y_ref[pl.ds(r0, c), h * GLA_DV:(h + 1) * GLA_DV] = o * _silu(zh)
        return carry

    lax.fori_loop(0, tc // c, chunk, 0)


def _gla(u_all, bsz, s_len, w_gk, b_gk, onorm, *, tc=256):
    tc = min(tc, s_len)
    ns = s_len // tc
    sm_blk = _U_LAYOUT['small'][0] // 128
    w_pad = jnp.zeros((128, GLA_HEADS * GLA_DK), F32).at[SM_GK:SM_GK + GLA_LOWRANK].set(w_gk.astype(F32))
    row = lambda b, s: b * ns + s
    return pl.pallas_call(
        functools.partial(_gla_kernel, tc=tc),
        grid=(bsz, ns),
        in_specs=[
            pl.BlockSpec((tc, 512), lambda b, s: (row(b, s), _U_LAYOUT['gla_q'][0] // 512)),
            pl.BlockSpec((tc, 512), lambda b, s: (row(b, s), _U_LAYOUT['gla_v'][0] // 512)),
            pl.BlockSpec((tc, 512), lambda b, s: (row(b, s), _U_LAYOUT['gla_z'][0] // 512)),
            pl.BlockSpec((tc, 128), lambda b, s: (row(b, s), sm_blk)),
            pl.BlockSpec((128, 256), lambda b, s: (0, 0)),
            pl.BlockSpec((1, 256), lambda b, s: (0, 0)),
            pl.BlockSpec((1, GLA_DV), lambda b, s: (0, 0)),
        ],
        out_specs=pl.BlockSpec((tc, 512), lambda b, s: (row(b, s), 0)),
        out_shape=jax.ShapeDtypeStruct((bsz * s_len, 512), F32),
        scratch_shapes=[pltpu.VMEM((GLA_HEADS, GLA_DV, GLA_DK), F32)],
        compiler_params=pltpu.CompilerParams(
            dimension_semantics=("arbitrary", "arbitrary"), vmem_limit_bytes=_VMEM_LIMIT),
        name="gla",
    )(u_all, u_all, u_all, u_all, w_pad, b_gk.reshape(1, -1).astype(F32), onorm.reshape(1, GLA_DV).astype(F32))


def _ssd_kernel(x_ref, bc_ref, z_ref, sm_ref, cw_ref, cb_ref, alog_ref, dtb_ref, dskip_ref, onorm_ref,
                y_ref, xbuf, state, *, tc):
    c = CHUNK
    hg = SSD_HEADS // SSD_GROUPS

    @pl.when(pl.program_id(1) == 0)
    def _():
        state[...] = jnp.zeros_like(state)
        xbuf[0:8, :] = jnp.zeros((8, xbuf.shape[1]), F32)

    xbuf[8:8 + tc, 0:512] = x_ref[...]
    xbuf[8:8 + tc, 512:1024] = bc_ref[...]

    ri, ci = _iota2((c, c), 0), _iota2((c, c), 1)
    tril = ri >= ci
    tril_f = tril.astype(F32)
    triu_f = (ri <= ci).astype(F32)
    ones_f = jnp.ones((c, c), F32)
    neg_a = -jnp.exp(alog_ref[...])
    dtb = dtb_ref[...]
    cbias = cb_ref[...]

    def chunk(ck, carry):
        r0 = pl.multiple_of(ck * c, c)
        dt_all = _softplus(sm_ref[pl.ds(r0, c), :] + dtb)
        da_all = dt_all * neg_a
        acs_all = _mm_hi(tril_f, da_all)
        sumsq = jnp.zeros((c, 1), F32)
        for g in range(SSD_GROUPS):
            bm = _causal_conv_silu(xbuf, r0, 512 + g * SSD_STATE, SSD_STATE, cw_ref,
                                   cbias[:, 512 + g * SSD_STATE:512 + (g + 1) * SSD_STATE])
            cm = _causal_conv_silu(xbuf, r0, 768 + g * SSD_STATE, SSD_STATE, cw_ref,
                                   cbias[:, 768 + g * SSD_STATE:768 + (g + 1) * SSD_STATE])
            cb = _mm_nt(cm, bm)
            for hh in range(g * hg, (g + 1) * hg):
                lo, hi = hh * SSD_HEADDIM, (hh + 1) * SSD_HEADDIM
                xs = _causal_conv_silu(xbuf, r0, lo, SSD_HEADDIM, cw_ref, cbias[:, lo:hi])
                col = SM_DT + hh
                dt_b = jnp.broadcast_to(dt_all[:, col:col + 1], (c, SSD_HEADDIM))
                da_b = jnp.broadcast_to(da_all[:, col:col + 1], (c, c))
                acs_b = jnp.broadcast_to(acs_all[:, col:col + 1], (c, c))
                acs_row = _mm_hi(ones_f, da_b * triu_f)
                lmat = jnp.exp(jnp.where(tril, acs_b - acs_row, -jnp.inf))
                xdt = xs * dt_b
                acs_last = acs_b[c - 1:c, :]
                st = state[hh]
                y = _mm(cb * lmat, xdt) + _mm_nt(cm, st) * jnp.exp(acs_b) + xs * dskip_ref[:, lo:hi]
                state[hh] = st * jnp.exp(acs_last[:, 0:1]) + _mm_tn(xdt * jnp.exp(acs_last - acs_b), bm)
                y = y * _silu(z_ref[pl.ds(r0, c), lo:hi])
                sumsq = sumsq + jnp.sum(y * y, axis=-1, keepdims=True)
                y_ref[pl.ds(r0, c), lo:hi] = y
        scale = lax.rsqrt(sumsq * (1.0 / (SSD_HEADS * SSD_HEADDIM)) + NORM_EPS)
        y_ref[pl.ds(r0, c), :] = y_ref[pl.ds(r0, c), :] * scale * onorm_ref[...]
        return carry

    lax.fori_loop(0, tc // c, chunk, 0)
    xbuf[0:8, 0:512] = x_ref[tc - 8:tc, :]
    xbuf[0:8, 512:1024] = bc_ref[tc - 8:tc, :]


def _ssd(u_all, bsz, s_len, conv_w, conv_b, a_log, dt_bias, d_skip, onorm, *, tc=256):
    tc = min(tc, s_len)
    ns = s_len // tc
    sm_blk = _U_LAYOUT['small'][0] // 128
    row = lambda b, s: b * ns + s
    const = lambda b, s: (0, 0)
    return pl.pallas_call(
        functools.partial(_ssd_kernel, tc=tc),
        grid=(bsz, ns),
        in_specs=[
            pl.BlockSpec((tc, 512), lambda b, s: (row(b, s), _U_LAYOUT['ssd_x'][0] // 512)),
            pl.BlockSpec((tc, 512), lambda b, s: (row(b, s), _U_LAYOUT['ssd_b'][0] // 512)),
            pl.BlockSpec((tc, 512), lambda b, s: (row(b, s), _U_LAYOUT['ssd_z'][0] // 512)),
            pl.BlockSpec((tc, 128), lambda b, s: (row(b, s), sm_blk)),
            pl.BlockSpec((CONV_K, 1024), const),
            pl.BlockSpec((1, 1024), const),
            pl.BlockSpec((1, 128), const),
            pl.BlockSpec((1, 128), const),
            pl.BlockSpec((1, 512), const),
            pl.BlockSpec((1, 512), const),
        ],
        out_specs=pl.BlockSpec((tc, 512), lambda b, s: (row(b, s), 0)),
        out_shape=jax.ShapeDtypeStruct((bsz * s_len, 512), F32),
        scratch_shapes=[pltpu.VMEM((tc + 8, 1024), F32), pltpu.VMEM((SSD_HEADS, SSD_HEADDIM, SSD_STATE), F32)],
        compiler_params=pltpu.CompilerParams(
            dimension_semantics=("arbitrary", "arbitrary"), vmem_limit_bytes=_VMEM_LIMIT),
        name="ssd",
    )(u_all, u_all, u_all, u_all, conv_w.astype(F32), conv_b.reshape(1, -1).astype(F32),
      _small_row(a_log, SM_DT), _small_row(dt_bias, SM_DT),
      jnp.repeat(d_skip.astype(F32), SSD_HEADDIM).reshape(1, 512), onorm.reshape(1, 512).astype(F32))


def _compress_kernel(a_ref, pa_ref, pb_ref, w1a_ref, w1b_ref, w2_ref, o_ref):
    a = a_ref[0, 0]
    nsub = a.shape[0]
    e = _mm(a + pa_ref[0], w1a_ref[0])
    f = _mm(a + pb_ref[0], w1b_ref[0])
    pre = e + pltpu.roll(f, nsub - 1, 0)
    o_ref[0, 0] = _mm(_silu(pre), w2_ref[0])


def _compress(sub, pos, w1, w2):
    _, bg, nsub, wid = sub.shape
    hd = NSA_HEADDIM
    pa = pos[:, :CMP_STRIDE].reshape(2, 1, wid).astype(F32)
    pb = pos[:, CMP_STRIDE:].reshape(2, 1, wid).astype(F32)
    w1a = w1[:, :wid].astype(_MXU)
    w1b = w1[:, wid:].astype(_MXU)
    sel = lambda kv, i: (kv, 0, 0)
    return pl.pallas_call(
        _compress_kernel,
        grid=(2, bg),
        in_specs=[
            pl.BlockSpec((1, 1, nsub, wid), lambda kv, i: (kv, i, 0, 0)),
            pl.BlockSpec((1, 1, wid), sel),
            pl.BlockSpec((1, 1, wid), sel),
            pl.BlockSpec((1, wid, hd), sel),
            pl.BlockSpec((1, wid, hd), sel),
            pl.BlockSpec((1, hd, hd), sel),
        ],
        out_specs=pl.BlockSpec((1, 1, nsub, hd), lambda kv, i: (kv, i, 0, 0)),
        out_shape=jax.ShapeDtypeStruct((2, bg, nsub, hd), F32),
        compiler_params=pltpu.CompilerParams(
            dimension_semantics=("arbitrary", "arbitrary"), vmem_limit_bytes=_VMEM_LIMIT),
        name="nsa_compress",
    )(sub, pa, pb, w1a, w1b, w2.astype(_MXU))


def _cmpattn_kernel(slopes_ref, q_ref, kc_ref, vc_ref, ovt_ref, oc_ref, qsel_ref, *, tq, nsel, nc, topk):
    hd = NSA_HEADDIM
    hg = NSA_HEADS // NSA_KV_HEADS
    g = pl.program_id(0) % NSA_KV_HEADS
    q0 = pl.program_id(1) * tq
    kc = kc_ref[0, 0]
    vc = vc_ref[0, 0]
    nsub = kc.shape[0]
    t = q0 + _iota2((tq, nsub), 0)
    ccol = _iota2((tq, nsub), 1)
    cpos = ccol * CMP_STRIDE + (CMP_BLOCK - 1)
    valid = (cpos <= t) & (ccol < nc)
    dist = (t - cpos).astype(F32)
    psum = jnp.zeros((tq, nsub), F32)
    for h in range(hg):
        slope = slopes_ref[g * hg + h]
        s = _mm_nt(q_ref[0, h][:, :hd], kc) - slope * dist
        s = jnp.where(valid, s, -jnp.inf)
        m = jnp.max(s, axis=-1, keepdims=True)
        m = jnp.where(m > -jnp.inf, m, 0.0)
        e = jnp.exp(s - m)
        p = e / jnp.maximum(jnp.sum(e, axis=-1, keepdims=True), 1e-30)
        oc_ref[0, h] = _mm(p, vc)
        psum = psum + p
    imp = lax.dot_general(ovt_ref[...], psum, (((1,), (1,)), ((), ())), preferred_element_type=F32, precision=_HI)
    jj = _iota2((nsel, tq), 0)
    cur = (q0 + _iota2((nsel, tq), 1)) // SEL_BLOCK
    forced = (jj == 0) | (jj == cur) | (jj == cur - 1)
    imp = jnp.where(jj <= cur, jnp.where(forced, FORCED_SCORE, imp), -1.0)
    rank = jnp.zeros((nsel, tq), jnp.int32)
    for j2 in range(nsel):
        row = imp[j2:j2 + 1, :]
        before = (row > imp) | ((row == imp) & (jj > j2))
        rank = rank + before.astype(jnp.int32)
    picked = (rank < topk) & (imp >= 0.0)
    bias_t = jnp.where(picked, 0.0, _UNSELECTED)
    pieces = [jnp.zeros((hd, tq), F32), bias_t]
    if nsel < hd:
        pieces.append(jnp.zeros((hd - nsel, tq), F32))
    bias = jnp.concatenate(pieces, axis=0).T
    for h in range(hg):
        qsel_ref[0, h] = (q_ref[0, h].astype(F32) + bias).astype(qsel_ref.dtype)


def _cmpattn(slopes, q_pad, kc, vc, ovt, *, nc, tq=256):
    bg, hg, s_len, _ = q_pad.shape
    tq = min(tq, s_len)
    nsel = s_len // SEL_BLOCK
    nsub = kc.shape[2]
    hd = NSA_HEADDIM
    return pl.pallas_call(
        functools.partial(_cmpattn_kernel, tq=tq, nsel=nsel, nc=nc, topk=min(SEL_TOPK, nsel)),
        grid=(bg, s_len // tq),
        in_specs=[
            pl.BlockSpec(memory_space=pltpu.SMEM),
            pl.BlockSpec((1, hg, tq, 2 * hd), lambda i, j: (i, 0, j, 0)),
            pl.BlockSpec((1, 1, nsub, hd), lambda i, j: (0, i, 0, 0)),
            pl.BlockSpec((1, 1, nsub, hd), lambda i, j: (1, i, 0, 0)),
            pl.BlockSpec((nsel, nsub), lambda i, j: (0, 0)),
        ],
        out_specs=[
            pl.BlockSpec((1, hg, tq, hd), lambda i, j: (i, 0, j, 0)),
            pl.BlockSpec((1, hg, tq, 2 * hd), lambda i, j: (i, 0, j, 0)),
        ],
        out_shape=[
            jax.ShapeDtypeStruct((bg, hg, s_len, hd), F32),
            jax.ShapeDtypeStruct((bg, hg, s_len, 2 * hd), _MXU),
        ],
        compiler_params=pltpu.CompilerParams(
            dimension_semantics=("arbitrary", "arbitrary"), vmem_limit_bytes=_VMEM_LIMIT),
        name="nsa_cmp_attn",
    )(slopes, q_pad, kc, vc, ovt)


def _flash_head(qh, k_ref, v_ref, slope, q0, kt_lo, kt_hi, *, tq, tk, window):
    rel = _iota2((tq, tk), 0) - _iota2((tq, tk), 1)

    def body(kt, carry):
        m, l, acc = carry
        k0 = pl.multiple_of(kt * tk, tk)
        kb = k_ref[0, pl.ds(k0, tk), :]
        vb = v_ref[0, pl.ds(k0, tk), :]
        d = rel + (q0 - k0)
        s = _mm_nt(qh, kb) - slope * d.astype(F32)
        ok = d >= 0
        if window is not None:
            ok = ok & (d < window)
        s = jnp.where(ok, s, _MASKED)
        m_new = jnp.maximum(m, jnp.max(s, axis=-1, keepdims=True))
        alpha = jnp.exp(m - m_new)
        p = jnp.exp(s - m_new)
        l = alpha * l + jnp.sum(p, axis=-1, keepdims=True)
        acc = alpha * acc + _mm(p, vb)
        return m_new, l, acc

    init = (jnp.full((tq, 1), _MASKED, F32), jnp.zeros((tq, 1), F32), jnp.zeros((tq, NSA_HEADDIM), F32))
    _, l, acc = lax.fori_loop(kt_lo, kt_hi, body, init)
    return acc / l


def _selattn_kernel(slopes_ref, q_ref, k_ref, v_ref, o_ref, *, tq, tk):
    hg = NSA_HEADS // NSA_KV_HEADS
    g = pl.program_id(0) % NSA_KV_HEADS
    q0 = pl.program_id(1) * tq
    kt_hi = (q0 + tq + tk - 1) // tk
    for h in range(hg):
        o_ref[0, h] = _flash_head(q_ref[0, h], k_ref, v_ref, slopes_ref[g * hg + h], q0, 0, kt_hi,
                                  tq=tq, tk=tk, window=None)


def _selattn(slopes, qsel, ksel, vsel, *, tq=128, tk=256):
    bg, hg, s_len, wid = qsel.shape
    tq, tk = min(tq, s_len), min(tk, s_len)
    hd = NSA_HEADDIM
    return pl.pallas_call(
        functools.partial(_selattn_kernel, tq=tq, tk=tk),
        grid=(bg, s_len // tq),
        in_specs=[
            pl.BlockSpec(memory_space=pltpu.SMEM),
            pl.BlockSpec((1, hg, tq, wid), lambda i, j: (i, 0, j, 0)),
            pl.BlockSpec((1, s_len, wid), lambda i, j: (i, 0, 0)),
            pl.BlockSpec((1, s_len, hd), lambda i, j: (i, 0, 0)),
        ],
        out_specs=pl.BlockSpec((1, hg, tq, hd), lambda i, j: (i, 0, j, 0)),
        out_shape=jax.ShapeDtypeStruct((bg, hg, s_len, hd), F32),
        compiler_params=pltpu.CompilerParams(
            dimension_semantics=("arbitrary", "arbitrary"), vmem_limit_bytes=_VMEM_LIMIT),
        name="nsa_sel_attn",
    )(slopes, qsel, ksel, vsel)


def _winattn_kernel(slopes_ref, q_ref, k_ref, v_ref, oc_ref, os_ref, gate_ref, z_ref, y_ref, *, tq, tk):
    hd = NSA_HEADDIM
    hg = NSA_HEADS // NSA_KV_HEADS
    g = pl.program_id(0) % NSA_KV_HEADS
    q0 = pl.program_id(1) * tq
    kt_lo = jnp.maximum(q0 - (WINDOW - 1), 0) // tk
    kt_hi = (q0 + tq + tk - 1) // tk
    gates = jax.nn.sigmoid(gate_ref[0])
    for h in range(hg):
        o_w = _flash_head(q_ref[0, h][:, :hd], k_ref, v_ref, slopes_ref[g * hg + h], q0, kt_lo, kt_hi,
                          tq=tq, tk=tk, window=WINDOW)
        gcol = lambda c: jnp.broadcast_to(gates[:, 3 * h + c:3 * h + c + 1], (tq, hd))
        o = gcol(0) * oc_ref[0, h] + gcol(1) * os_ref[0, h] + gcol(2) * o_w
        y_ref[:, h * hd:(h + 1) * hd] = o * _silu(z_ref[:, h * hd:(h + 1) * hd])


def _winattn(slopes, q_pad, kwin, vwin, o_c, o_s, gates, u_all, *, tq=128, tk=128):
    bg, hg, s_len, wid = q_pad.shape
    tq, tk = min(tq, s_len), min(tk, s_len)
    hd = NSA_HEADDIM
    nq = s_len // tq
    gw = hg * hd
    zblk = _U_LAYOUT['nsa_z'][0] // gw
    ngrp = NSA_KV_HEADS
    return pl.pallas_call(
        functools.partial(_winattn_kernel, tq=tq, tk=tk),
        grid=(bg, nq),
        in_specs=[
            pl.BlockSpec(memory_space=pltpu.SMEM),
            pl.BlockSpec((1, hg, tq, wid), lambda i, j: (i, 0, j, 0)),
            pl.BlockSpec((1, s_len, hd), lambda i, j: (i, 0, 0)),
            pl.BlockSpec((1, s_len, hd), lambda i, j: (i, 0, 0)),
            pl.BlockSpec((1, hg, tq, hd), lambda i, j: (i, 0, j, 0)),
            pl.BlockSpec((1, hg, tq, hd), lambda i, j: (i, 0, j, 0)),
            pl.BlockSpec((1, tq, 128), lambda i, j: (i, j, 0)),
            pl.BlockSpec((tq, gw), lambda i, j: ((i // ngrp) * nq + j, zblk + i % ngrp)),
        ],
        out_specs=pl.BlockSpec((tq, gw), lambda i, j: ((i // ngrp) * nq + j, i % ngrp)),
        out_shape=jax.ShapeDtypeStruct((bg // ngrp * s_len, ngrp * gw), F32),
        compiler_params=pltpu.CompilerParams(
            dimension_semantics=("arbitrary", "arbitrary"), vmem_limit_bytes=_VMEM_LIMIT),
        name="nsa_win_attn",
    )(slopes, q_pad, kwin, vwin, o_c, o_s, gates, u_all)


def _nsa(u_all, bsz, s_len, cmp_pos_k, cmp_pos_v, w_ck1, w_ck2, w_cv1, w_cv2):
    ng, hg, hd = NSA_KV_HEADS, NSA_HEADS // NSA_KV_HEADS, NSA_HEADDIM
    bg = bsz * ng
    seg = lambda name: lax.slice_in_dim(u_all, _U_LAYOUT[name][0], _U_LAYOUT[name][0] + _U_LAYOUT[name][1], axis=1)
    kvh = lambda name: seg(name).reshape(bsz, s_len, ng, hd).transpose(0, 2, 1, 3).reshape(bg, s_len, hd)
    q = seg('nsa_q').reshape(bsz, s_len, ng, hg, hd).transpose(0, 2, 3, 1, 4).reshape(bg, hg, s_len, hd)
    q_pad = jnp.pad(q * (hd ** -0.5), ((0, 0), (0, 0), (0, 0), (0, hd))).astype(_MXU)
    nsel = s_len // SEL_BLOCK
    onehot = (jnp.arange(s_len)[:, None] // SEL_BLOCK == jnp.arange(hd)[None, :]).astype(F32)
    ksel = jnp.concatenate([kvh('nsa_ks'), jnp.broadcast_to(onehot, (bg, s_len, hd))], axis=-1).astype(_MXU)
    vsel = kvh('nsa_vs').astype(_MXU)
    kwin = kvh('nsa_kw').astype(_MXU)
    vwin = kvh('nsa_vw').astype(_MXU)
    nsub = s_len // CMP_STRIDE
    nc = nsub - CMP_BLOCK // CMP_STRIDE + 1
    sub = jnp.stack([kvh('nsa_kc'), kvh('nsa_vc')]).reshape(2, bg, nsub, CMP_STRIDE * hd)
    cmp = _compress(sub, jnp.stack([cmp_pos_k, cmp_pos_v]), jnp.stack([w_ck1, w_cv1]), jnp.stack([w_ck2, w_cv2]))
    cmp_start = jnp.arange(nsub) * CMP_STRIDE
    sel_start = jnp.arange(nsel) * SEL_BLOCK
    ovt = ((cmp_start[None, :] <= sel_start[:, None] + SEL_BLOCK - 1)
           & (cmp_start[None, :] + CMP_BLOCK - 1 >= sel_start[:, None])
           & (jnp.arange(nsub)[None, :] < nc)).astype(F32)
    slopes = 2.0 ** (-8.0 * jnp.arange(1, NSA_HEADS + 1, dtype=F32) / NSA_HEADS)
    o_c, qsel = _cmpattn(slopes, q_pad, cmp, cmp, ovt, nc=nc)
    o_s = _selattn(slopes, qsel, ksel, vsel)
    gsm = seg('small')[:, SM_GATE:SM_GATE + 3 * NSA_HEADS]
    gates = gsm.reshape(bsz, s_len, ng, 3 * hg).transpose(0, 2, 1, 3).reshape(bg, s_len, 3 * hg)
    gates = jnp.pad(gates, ((0, 0), (0, 0), (0, 128 - 3 * hg)))
    return _winattn(slopes, q_pad, kwin, vwin, o_c, o_s, gates, u_all)


def _merge_kernel(y0_ref, y1_ref, y2_ref, y3_ref, g0_ref, g1_ref, g2_ref, g3_ref, wbr_ref, wout_ref, gain_ref,
                  x_ref, o_ref):
    merged = None
    for n, (y_ref, g_ref) in enumerate(((y0_ref, g0_ref), (y1_ref, g1_ref), (y2_ref, g2_ref), (y3_ref, g3_ref))):
        term = jax.nn.sigmoid(g_ref[...]) * jnp.dot(y_ref[...].astype(_MXU), wbr_ref[n], preferred_element_type=F32)
        merged = term if merged is None else merged + term
    out = jnp.dot(merged.astype(_MXU), wout_ref[...], preferred_element_type=F32)
    out = out * lax.rsqrt(jnp.mean(out * out, axis=-1, keepdims=True) + NORM_EPS) * gain_ref[...]
    o_ref[...] = x_ref[...] + out


def _merge(ys, u_all, w_br, w_out, gain, x2d, *, tm=256):
    t = x2d.shape[0]
    tm = min(tm, t)
    gblk = _U_LAYOUT['merge_gate'][0] // D_MODEL
    yspec = pl.BlockSpec((tm, BR_WIDTH), lambda i: (i, 0))
    gspec = lambda n: pl.BlockSpec((tm, D_MODEL), lambda i: (i, gblk + n))
    return pl.pallas_call(
        _merge_kernel,
        grid=(t // tm,),
        in_specs=[yspec] * 4 + [gspec(n) for n in range(4)] + [
            pl.BlockSpec((N_BRANCH, BR_WIDTH, D_MODEL), lambda i: (0, 0, 0)),
            pl.BlockSpec((D_MODEL, D_MODEL), lambda i: (0, 0)),
            pl.BlockSpec((1, D_MODEL), lambda i: (0, 0)),
            pl.BlockSpec((tm, D_MODEL), lambda i: (i, 0)),
        ],
        out_specs=pl.BlockSpec((tm, D_MODEL), lambda i: (i, 0)),
        out_shape=jax.ShapeDtypeStruct((t, D_MODEL), F32),
        compiler_params=pltpu.CompilerParams(dimension_semantics=("arbitrary",), vmem_limit_bytes=_VMEM_LIMIT),
        name="merge",
    )(*ys, u_all, u_all, u_all, u_all, w_br.astype(_MXU), w_out.astype(_MXU), gain.reshape(1, D_MODEL).astype(F32), x2d)


def _layer(x2d, bsz, s_len, p):
    u_all = _inproj(x2d, p['norm_pre'].astype(F32), _layout_weight(p['w_in']).astype(_MXU))
    ys = (
        _gdn(u_all, bsz, s_len, p['conv_a'], p['a_log_a'], p['dt_bias_a'], p['onorm_a']),
        _gla(u_all, bsz, s_len, p['w_gk'], p['b_gk'], p['onorm_b']),
        _ssd(u_all, bsz, s_len, p['conv_c'], p['conv_bias_c'], p['a_log_c'], p['dt_bias_c'], p['d_skip_c'], p['onorm_c']),
        _nsa(u_all, bsz, s_len, p['cmp_pos_k'], p['cmp_pos_v'], p['w_ck1'], p['w_ck2'], p['w_cv1'], p['w_cv2']),
    )
    return _merge(ys, u_all, p['w_br'], p['w_out'], p['norm_post'], x2d)


def kernel(x, norm_pre, norm_post, w_in, conv_a, a_log_a, dt_bias_a, onorm_a, w_gk, b_gk, onorm_b, conv_c, conv_bias_c, a_log_c, dt_bias_c, d_skip_c, onorm_c, cmp_pos_k, cmp_pos_v, w_ck1, w_ck2, w_cv1, w_cv2, w_br, w_out):
    bsz, s_len, _ = x.shape
    params = dict(norm_pre=norm_pre, norm_post=norm_post, w_in=w_in, conv_a=conv_a, a_log_a=a_log_a,
                  dt_bias_a=dt_bias_a, onorm_a=onorm_a, w_gk=w_gk, b_gk=b_gk, onorm_b=onorm_b, conv_c=conv_c,
                  conv_bias_c=conv_bias_c, a_log_c=a_log_c, dt_bias_c=dt_bias_c, d_skip_c=d_skip_c,
                  onorm_c=onorm_c, cmp_pos_k=cmp_pos_k, cmp_pos_v=cmp_pos_v, w_ck1=w_ck1, w_ck2=w_ck2,
                  w_cv1=w_cv1, w_cv2=w_cv2, w_br=w_br, w_out=w_out)
    x2d = x.reshape(bsz * s_len, D_MODEL)
    for l in range(w_in.shape[0]):
        x2d = _layer(x2d, bsz, s_len, {k: v[l] for k, v in params.items()})
    return x2d.reshape(bsz, s_len, D_MODEL)
```

```python
import functools

import jax
import jax.numpy as jnp
from jax import lax
from jax.experimental import pallas as pl
from jax.experimental.pallas import tpu as pltpu

D_MODEL = 1024
N_BRANCH = 4
BR_WIDTH = 512
CONV_K = 4
NORM_EPS = 1e-6

GDN_HEADS = 4
GDN_DK = 128
GDN_DV = 128
GLA_HEADS = 4
GLA_DK = 64
GLA_DV = 128
GLA_LOWRANK = 16
GLA_GATE_NORMALIZER = 16.0
SSD_HEADS = 8
SSD_HEADDIM = 64
SSD_STATE = 128
SSD_GROUPS = 2
NSA_HEADS = 8
NSA_KV_HEADS = 2
NSA_HEADDIM = 64
CMP_BLOCK = 32
CMP_STRIDE = 16
SEL_BLOCK = 64
SEL_TOPK = 16
WINDOW = 512
FORCED_SCORE = 1e4
CHUNK = 64

F32 = jnp.float32
BF16 = jnp.bfloat16
_MXU = jnp.bfloat16
_HI = lax.Precision.HIGHEST
_VMEM_LIMIT = 56 * 1024 * 1024
_MASKED = -1e30
_UNSELECTED = -1e30
_SEL_TQ, _SEL_TK = 128, 256
_WIN_TQ, _WIN_TK = 128, 128

_U_LAYOUT = {}
_off = 0
for _name, _w in (
    ('gdn_q', 512), ('gdn_k', 512), ('gdn_v', 512), ('gdn_z', 512),
    ('gla_q', 256), ('gla_k', 256), ('gla_v', 512), ('gla_z', 512),
    ('ssd_x', 512), ('ssd_b', 256), ('ssd_c', 256), ('ssd_z', 512),
    ('nsa_q', 512), ('nsa_kc', 128), ('nsa_vc', 128), ('nsa_ks', 128), ('nsa_vs', 128),
    ('nsa_kw', 128), ('nsa_vw', 128), ('nsa_z', 512),
    ('small', 128), ('pad', 128), ('merge_gate', 4096),
):
    _U_LAYOUT[_name] = (_off, _w)
    _off += _w
U_WIDTH = _off
SM_BETA, SM_A, SM_GK, SM_DT, SM_GATE = 0, 4, 8, 24, 32

_IN_SPLITS = (
    ('gdn_q', 512), ('gdn_k', 512), ('gdn_v', 512), ('gdn_beta', 4), ('gdn_a', 4), ('gdn_z', 512),
    ('gla_q', 256), ('gla_k', 256), ('gla_v', 512), ('gla_gk', 16), ('gla_z', 512),
    ('ssd_x', 512), ('ssd_b', 256), ('ssd_c', 256), ('ssd_dt', 8), ('ssd_z', 512),
    ('nsa_q', 512), ('nsa_kc', 128), ('nsa_vc', 128), ('nsa_ks', 128), ('nsa_vs', 128),
    ('nsa_kw', 128), ('nsa_vw', 128), ('nsa_gate', 24), ('nsa_z', 512),
    ('merge_gate', 4096),
)
_SMALL_COL = {'gdn_beta': SM_BETA, 'gdn_a': SM_A, 'gla_gk': SM_GK, 'ssd_dt': SM_DT, 'nsa_gate': SM_GATE}


def _layout_weight(w_in_l):
    out = jnp.zeros((w_in_l.shape[0], U_WIDTH), w_in_l.dtype)
    start = 0
    for name, width in _IN_SPLITS:
        if name in _SMALL_COL:
            dst = _U_LAYOUT['small'][0] + _SMALL_COL[name]
        else:
            dst = _U_LAYOUT[name][0]
        out = lax.dynamic_update_slice(out, w_in_l[:, start:start + width], (0, dst))
        start += width
    return out


def _mm(a, b):
    return jnp.dot(a.astype(_MXU), b.astype(_MXU), preferred_element_type=F32)


def _mm_nt(a, b):
    return lax.dot_general(a.astype(_MXU), b.astype(_MXU), (((1,), (1,)), ((), ())),
                           preferred_element_type=F32)


def _mm_tn(a, b):
    return lax.dot_general(a.astype(_MXU), b.astype(_MXU), (((0,), (0,)), ((), ())),
                           preferred_element_type=F32)


def _mm_hi(a, b):
    return jnp.dot(a, b, preferred_element_type=F32, precision=_HI)


def _silu(x):
    return x * jax.nn.sigmoid(x)


def _softplus(x):
    return jnp.maximum(x, 0.0) + jnp.log(1.0 + jnp.exp(-jnp.abs(x)))


def _iota2(shape, axis):
    return lax.broadcasted_iota(jnp.int32, shape, axis)


def _causal_conv_silu(xbuf, r0, col, width, w_ref, bias=None):
    xx = xbuf[pl.ds(r0, CHUNK + 8), col:col + width]
    w = w_ref[:, col:col + width]
    acc = xx * w[CONV_K - 1:CONV_K, :]
    for k in range(1, CONV_K):
        acc = acc + pltpu.roll(xx, k, 0) * w[CONV_K - 1 - k:CONV_K - k, :]
    y = acc[8:, :]
    if bias is not None:
        y = y + bias
    return _silu(y)


def _inproj_kernel(x_ref, g_ref, w_ref, o_ref, h_ref):
    @pl.when(pl.program_id(1) == 0)
    def _():
        x = x_ref[...]
        ms = jnp.mean(x * x, axis=-1, keepdims=True)
        h_ref[...] = (x * lax.rsqrt(ms + NORM_EPS) * g_ref[...]).astype(h_ref.dtype)

    o_ref[...] = jnp.dot(h_ref[...], w_ref[...], preferred_element_type=F32)


def _inproj(x2d, gain, w_all, *, tm=1024, tn=1408):
    t = x2d.shape[0]
    tm = min(tm, t)
    return pl.pallas_call(
        _inproj_kernel,
        grid=(t // tm, U_WIDTH // tn),
        in_specs=[
            pl.BlockSpec((tm, D_MODEL), lambda i, j: (i, 0)),
            pl.BlockSpec((1, D_MODEL), lambda i, j: (0, 0)),
            pl.BlockSpec((D_MODEL, tn), lambda i, j: (0, j)),
        ],
        out_specs=pl.BlockSpec((tm, tn), lambda i, j: (i, j)),
        out_shape=jax.ShapeDtypeStruct((t, U_WIDTH), F32),
        scratch_shapes=[pltpu.VMEM((tm, D_MODEL), _MXU)],
        compiler_params=pltpu.CompilerParams(
            dimension_semantics=("arbitrary", "arbitrary"), vmem_limit_bytes=_VMEM_LIMIT),
        name="inproj",
    )(x2d, gain.reshape(1, D_MODEL), w_all)


def _gdn_kernel(qkv_ref, z_ref, sm_ref, cw_ref, alog_ref, dtb_ref, onorm_ref, y_ref, xbuf, state, *, tc):
    c = CHUNK

    @pl.when(pl.program_id(1) == 0)
    def _():
        state[...] = jnp.zeros_like(state)
        xbuf[0:8, :] = jnp.zeros((8, xbuf.shape[1]), F32)

    xbuf[8:8 + tc, :] = qkv_ref[...]

    ri, ci = _iota2((c, c), 0), _iota2((c, c), 1)
    tril = ri >= ci
    stril = ri > ci
    tril_f = tril.astype(F32)
    triu_f = (ri <= ci).astype(F32)
    eye_f = (ri == ci).astype(F32)
    ones_f = jnp.ones((c, c), F32)
    neg_a = -jnp.exp(alog_ref[...])
    dtb = dtb_ref[...]
    onorm = onorm_ref[...]

    def chunk(ck, carry):
        r0 = pl.multiple_of(ck * c, c)
        sm = sm_ref[pl.ds(r0, c), :]
        beta_all = jax.nn.sigmoid(sm)
        g_all = neg_a * _softplus(sm + dtb)
        for h in range(GDN_HEADS):
            q = _causal_conv_silu(xbuf, r0, h * GDN_DK, GDN_DK, cw_ref)
            k = _causal_conv_silu(xbuf, r0, 512 + h * GDN_DK, GDN_DK, cw_ref)
            v = _causal_conv_silu(xbuf, r0, 1024 + h * GDN_DV, GDN_DV, cw_ref)
            q = q * lax.rsqrt(jnp.sum(q * q, axis=-1, keepdims=True) + NORM_EPS) * (GDN_DK ** -0.5)
            k = k * lax.rsqrt(jnp.sum(k * k, axis=-1, keepdims=True) + NORM_EPS)
            beta = jnp.broadcast_to(beta_all[:, SM_BETA + h:SM_BETA + h + 1], (c, GDN_DK))
            gb = jnp.broadcast_to(g_all[:, SM_A + h:SM_A + h + 1], (c, GDN_DK))
            gc = _mm_hi(tril_f, gb)
            gc_row = _mm_hi(ones_f, gb[:, :c] * triu_f)
            decay = jnp.exp(jnp.where(tril, gc[:, :c] - gc_row, -jnp.inf))
            kb = k * beta
            lower = jnp.where(stril, _mm_nt(kb, k) * decay, 0.0)
            npow = -lower
            inv = eye_f + npow
            for _ in range(5):
                npow = _mm_hi(npow, npow)
                inv = inv + _mm_hi(inv, npow)
            egc = jnp.exp(gc)
            u_c = _mm_hi(inv, v * beta)
            w_c = _mm_hi(inv, kb * egc)
            attn = _mm_nt(q, k) * decay
            qg = q * egc
            gc_last = gc[c - 1:c, :]
            kd = k * jnp.exp(gc_last - gc)
            st = state[h]
            v_new = u_c - _mm(w_c, st)
            o = _mm(qg, st) + _mm(attn, v_new)
            state[h] = st * jnp.exp(gc_last) + _mm_tn(kd, v_new)
            o = o * lax.rsqrt(jnp.mean(o * o, axis=-1, keepdims=True) + NORM_EPS) * onorm
            zh = z_ref[pl.ds(r0, c), h * GDN_DV:(h + 1) * GDN_DV]
            y_ref[pl.ds(r0, c), h * GDN_DV:(h + 1) * GDN_DV] = o * _silu(zh)
        return carry

    lax.fori_loop(0, tc // c, chunk, 0)
    xbuf[0:8, :] = qkv_ref[tc - 8:tc, :]


def _small_row(vec, col):
    return jnp.zeros((1, 128), F32).at[0, col:col + vec.shape[0]].set(vec.astype(F32))


def _gdn(u_all, bsz, s_len, conv_w, a_log, dt_bias, onorm, *, tc=256):
    tc = min(tc, s_len)
    ns = s_len // tc
    sm_blk = _U_LAYOUT['small'][0] // 128
    return pl.pallas_call(
        functools.partial(_gdn_kernel, tc=tc),
        grid=(bsz, ns),
        in_specs=[
            pl.BlockSpec((tc, 1536), lambda b, s: (b * ns + s, 0)),
            pl.BlockSpec((tc, 512), lambda b, s: (b * ns + s, _U_LAYOUT['gdn_z'][0] // 512)),
            pl.BlockSpec((tc, 128), lambda b, s: (b * ns + s, sm_blk)),
            pl.BlockSpec((CONV_K, 1536), lambda b, s: (0, 0)),
            pl.BlockSpec((1, 128), lambda b, s: (0, 0)),
            pl.BlockSpec((1, 128), lambda b, s: (0, 0)),
            pl.BlockSpec((1, GDN_DV), lambda b, s: (0, 0)),
        ],
        out_specs=pl.BlockSpec((tc, 512), lambda b, s: (b * ns + s, 0)),
        out_shape=jax.ShapeDtypeStruct((bsz * s_len, 512), F32),
        scratch_shapes=[pltpu.VMEM((tc + 8, 1536), F32), pltpu.VMEM((GDN_HEADS, GDN_DK, GDN_DV), F32)],
        compiler_params=pltpu.CompilerParams(
            dimension_semantics=("arbitrary", "arbitrary"), vmem_limit_bytes=_VMEM_LIMIT),
        name="gdn",
    )(u_all, u_all, u_all, conv_w.astype(F32), _small_row(a_log, SM_A), _small_row(dt_bias, SM_A),
      onorm.reshape(1, GDN_DV).astype(F32))


def _gla_kernel(qk_ref, v_ref, z_ref, sm_ref, wgk_ref, bgk_ref, onorm_ref, y_ref, state_t, *, tc):
    c = CHUNK

    @pl.when(pl.program_id(1) == 0)
    def _():
        state_t[...] = jnp.zeros_like(state_t)

    ri, ci = _iota2((c, c), 0), _iota2((c, c), 1)
    tril = ri >= ci
    tril_f = tril.astype(F32)
    onorm = onorm_ref[...]

    def chunk(ck, carry):
        r0 = pl.multiple_of(ck * c, c)
        pre = _mm(sm_ref[pl.ds(r0, c), :], wgk_ref[...]) + bgk_ref[...]
        gk_all = (jnp.minimum(pre, 0.0) - jnp.log(1.0 + jnp.exp(-jnp.abs(pre)))) * (1.0 / GLA_GATE_NORMALIZER)
        for h in range(GLA_HEADS):
            q = qk_ref[pl.ds(r0, c), h * GLA_DK:(h + 1) * GLA_DK] * (GLA_DK ** -0.5)
            k = qk_ref[pl.ds(r0, c), 256 + h * GLA_DK:256 + (h + 1) * GLA_DK]
            v = v_ref[pl.ds(r0, c), h * GLA_DV:(h + 1) * GLA_DV]
            b = _mm_hi(tril_f, gk_all[:, h * GLA_DK:(h + 1) * GLA_DK])
            bref = b[c // 2:c // 2 + 1, :]
            b_last = b[c - 1:c, :]
            a_intra = jnp.where(tril, _mm_nt(q * jnp.exp(b - bref), k * jnp.exp(bref - b)), 0.0)
            st = state_t[h]
            o = _mm(a_intra, v) + _mm_nt(q * jnp.exp(b), st)
            state_t[h] = st * jnp.exp(b_last) + _mm_tn(v, k * jnp.exp(b_last - b))
            o = o * lax.rsqrt(jnp.mean(o * o, axis=-1, keepdims=True) + NORM_EPS) * onorm
            zh = z_ref[pl.ds(r0, c), h * GLA_DV:(h + 1) * GLA_DV]
            y_ref[pl.ds(r0, c), h * GLA_DV:(h + 1) * GLA_DV] = o * _silu(zh)
        return carry

    lax.fori_loop(0, tc // c, chunk, 0)


def _gla(u_all, bsz, s_len, w_gk, b_gk, onorm, *, tc=256):
    tc = min(tc, s_len)
    ns = s_len // tc
    sm_blk = _U_LAYOUT['small'][0] // 128
    w_pad = jnp.zeros((128, GLA_HEADS * GLA_DK), F32).at[SM_GK:SM_GK + GLA_LOWRANK].set(w_gk.astype(F32))
    row = lambda b, s: b * ns + s
    return pl.pallas_call(
        functools.partial(_gla_kernel, tc=tc),
        grid=(bsz, ns),
        in_specs=[
            pl.BlockSpec((tc, 512), lambda b, s: (row(b, s), _U_LAYOUT['gla_q'][0] // 512)),
            pl.BlockSpec((tc, 512), lambda b, s: (row(b, s), _U_LAYOUT['gla_v'][0] // 512)),
            pl.BlockSpec((tc, 512), lambda b, s: (row(b, s), _U_LAYOUT['gla_z'][0] // 512)),
            pl.BlockSpec((tc, 128), lambda b, s: (row(b, s), sm_blk)),
            pl.BlockSpec((128, 256), lambda b, s: (0, 0)),
            pl.BlockSpec((1, 256), lambda b, s: (0, 0)),
            pl.BlockSpec((1, GLA_DV), lambda b, s: (0, 0)),
        ],
        out_specs=pl.BlockSpec((tc, 512), lambda b, s: (row(b, s), 0)),
        out_shape=jax.ShapeDtypeStruct((bsz * s_len, 512), F32),
        scratch_shapes=[pltpu.VMEM((GLA_HEADS, GLA_DV, GLA_DK), F32)],
        compiler_params=pltpu.CompilerParams(
            dimension_semantics=("arbitrary", "arbitrary"), vmem_limit_bytes=_VMEM_LIMIT),
        name="gla",
    )(u_all, u_all, u_all, u_all, w_pad, b_gk.reshape(1, -1).astype(F32), onorm.reshape(1, GLA_DV).astype(F32))


def _ssd_kernel(x_ref, bc_ref, z_ref, sm_ref, cw_ref, cb_ref, alog_ref, dtb_ref, dskip_ref, onorm_ref,
                y_ref, xbuf, state, *, tc):
    c = CHUNK
    hg = SSD_HEADS // SSD_GROUPS

    @pl.when(pl.program_id(1) == 0)
    def _():
        state[...] = jnp.zeros_like(state)
        xbuf[0:8, :] = jnp.zeros((8, xbuf.shape[1]), F32)

    xbuf[8:8 + tc, 0:512] = x_ref[...]
    xbuf[8:8 + tc, 512:1024] = bc_ref[...]

    ri, ci = _iota2((c, c), 0), _iota2((c, c), 1)
    tril = ri >= ci
    tril_f = tril.astype(F32)
    triu_f = (ri <= ci).astype(F32)
    ones_f = jnp.ones((c, c), F32)
    neg_a = -jnp.exp(alog_ref[...])
    dtb = dtb_ref[...]
    cbias = cb_ref[...]

    def chunk(ck, carry):
        r0 = pl.multiple_of(ck * c, c)
        dt_all = _softplus(sm_ref[pl.ds(r0, c), :] + dtb)
        da_all = dt_all * neg_a
        acs_all = _mm_hi(tril_f, da_all)
        sumsq = jnp.zeros((c, 1), F32)
        for g in range(SSD_GROUPS):
            bm = _causal_conv_silu(xbuf, r0, 512 + g * SSD_STATE, SSD_STATE, cw_ref,
                                   cbias[:, 512 + g * SSD_STATE:512 + (g + 1) * SSD_STATE])
            cm = _causal_conv_silu(xbuf, r0, 768 + g * SSD_STATE, SSD_STATE, cw_ref,
                                   cbias[:, 768 + g * SSD_STATE:768 + (g + 1) * SSD_STATE])
            cb = _mm_nt(cm, bm)
            for hh in range(g * hg, (g + 1) * hg):
                lo, hi = hh * SSD_HEADDIM, (hh + 1) * SSD_HEADDIM
                xs = _causal_conv_silu(xbuf, r0, lo, SSD_HEADDIM, cw_ref, cbias[:, lo:hi])
                col = SM_DT + hh
                dt_b = jnp.broadcast_to(dt_all[:, col:col + 1], (c, SSD_HEADDIM))
                da_b = jnp.broadcast_to(da_all[:, col:col + 1], (c, c))
                acs_b = jnp.broadcast_to(acs_all[:, col:col + 1], (c, c))
                acs_row = _mm_hi(ones_f, da_b * triu_f)
                lmat = jnp.exp(jnp.where(tril, acs_b - acs_row, -jnp.inf))
                xdt = xs * dt_b
                acs_last = acs_b[c - 1:c, :]
                st = state[hh]
                y = _mm(cb * lmat, xdt) + _mm_nt(cm, st) * jnp.exp(acs_b) + xs * dskip_ref[:, lo:hi]
                state[hh] = st * jnp.exp(acs_last[:, 0:1]) + _mm_tn(xdt * jnp.exp(acs_last - acs_b), bm)
                y = y * _silu(z_ref[pl.ds(r0, c), lo:hi])
                sumsq = sumsq + jnp.sum(y * y, axis=-1, keepdims=True)
                y_ref[pl.ds(r0, c), lo:hi] = y
        scale = lax.rsqrt(sumsq * (1.0 / (SSD_HEADS * SSD_HEADDIM)) + NORM_EPS)
        y_ref[pl.ds(r0, c), :] = y_ref[pl.ds(r0, c), :] * scale * onorm_ref[...]
        return carry

    lax.fori_loop(0, tc // c, chunk, 0)
    xbuf[0:8, 0:512] = x_ref[tc - 8:tc, :]
    xbuf[0:8, 512:1024] = bc_ref[tc - 8:tc, :]


def _ssd(u_all, bsz, s_len, conv_w, conv_b, a_log, dt_bias, d_skip, onorm, *, tc=256):
    tc = min(tc, s_len)
    ns = s_len // tc
    sm_blk = _U_LAYOUT['small'][0] // 128
    row = lambda b, s: b * ns + s
    const = lambda b, s: (0, 0)
    return pl.pallas_call(
        functools.partial(_ssd_kernel, tc=tc),
        grid=(bsz, ns),
        in_specs=[
            pl.BlockSpec((tc, 512), lambda b, s: (row(b, s), _U_LAYOUT['ssd_x'][0] // 512)),
            pl.BlockSpec((tc, 512), lambda b, s: (row(b, s), _U_LAYOUT['ssd_b'][0] // 512)),
            pl.BlockSpec((tc, 512), lambda b, s: (row(b, s), _U_LAYOUT['ssd_z'][0] // 512)),
            pl.BlockSpec((tc, 128), lambda b, s: (row(b, s), sm_blk)),
            pl.BlockSpec((CONV_K, 1024), const),
            pl.BlockSpec((1, 1024), const),
            pl.BlockSpec((1, 128), const),
            pl.BlockSpec((1, 128), const),
            pl.BlockSpec((1, 512), const),
            pl.BlockSpec((1, 512), const),
        ],
        out_specs=pl.BlockSpec((tc, 512), lambda b, s: (row(b, s), 0)),
        out_shape=jax.ShapeDtypeStruct((bsz * s_len, 512), F32),
        scratch_shapes=[pltpu.VMEM((tc + 8, 1024), F32), pltpu.VMEM((SSD_HEADS, SSD_HEADDIM, SSD_STATE), F32)],
        compiler_params=pltpu.CompilerParams(
            dimension_semantics=("arbitrary", "arbitrary"), vmem_limit_bytes=_VMEM_LIMIT),
        name="ssd",
    )(u_all, u_all, u_all, u_all, conv_w.astype(F32), conv_b.reshape(1, -1).astype(F32),
      _small_row(a_log, SM_DT), _small_row(dt_bias, SM_DT),
      jnp.repeat(d_skip.astype(F32), SSD_HEADDIM).reshape(1, 512), onorm.reshape(1, 512).astype(F32))


def _compress_kernel(a_ref, pa_ref, pb_ref, w1a_ref, w1b_ref, w2_ref, o_ref):
    a = a_ref[0, 0]
    nsub = a.shape[0]
    e = _mm(a + pa_ref[0], w1a_ref[0])
    f = _mm(a + pb_ref[0], w1b_ref[0])
    pre = e + pltpu.roll(f, nsub - 1, 0)
    o_ref[0, 0] = _mm(_silu(pre), w2_ref[0])


def _compress(sub, pos, w1, w2):
    _, bg, nsub, wid = sub.shape
    hd = NSA_HEADDIM
    pa = pos[:, :CMP_STRIDE].reshape(2, 1, wid).astype(F32)
    pb = pos[:, CMP_STRIDE:].reshape(2, 1, wid).astype(F32)
    w1a = w1[:, :wid].astype(_MXU)
    w1b = w1[:, wid:].astype(_MXU)
    sel = lambda kv, i: (kv, 0, 0)
    return pl.pallas_call(
        _compress_kernel,
        grid=(2, bg),
        in_specs=[
            pl.BlockSpec((1, 1, nsub, wid), lambda kv, i: (kv, i, 0, 0)),
            pl.BlockSpec((1, 1, wid), sel),
            pl.BlockSpec((1, 1, wid), sel),
            pl.BlockSpec((1, wid, hd), sel),
            pl.BlockSpec((1, wid, hd), sel),
            pl.BlockSpec((1, hd, hd), sel),
        ],
        out_specs=pl.BlockSpec((1, 1, nsub, hd), lambda kv, i: (kv, i, 0, 0)),
        out_shape=jax.ShapeDtypeStruct((2, bg, nsub, hd), F32),
        compiler_params=pltpu.CompilerParams(
            dimension_semantics=("arbitrary", "arbitrary"), vmem_limit_bytes=_VMEM_LIMIT),
        name="nsa_compress",
    )(sub, pa, pb, w1a, w1b, w2.astype(_MXU))


def _cmpattn_kernel(slopes_ref, q_ref, kc_ref, vc_ref, ovt_ref, oc_ref, qsel_ref, *, tq, nsel, nc, topk):
    hd = NSA_HEADDIM
    hg = NSA_HEADS // NSA_KV_HEADS
    g = pl.program_id(0) % NSA_KV_HEADS
    q0 = pl.program_id(1) * tq
    kc = kc_ref[0, 0]
    vc = vc_ref[0, 0]
    nsub = kc.shape[0]
    t = q0 + _iota2((tq, nsub), 0)
    ccol = _iota2((tq, nsub), 1)
    cpos = ccol * CMP_STRIDE + (CMP_BLOCK - 1)
    valid = (cpos <= t) & (ccol < nc)
    dist = (t - cpos).astype(F32)
    psum = jnp.zeros((tq, nsub), F32)
    for h in range(hg):
        slope = slopes_ref[g * hg + h]
        s = _mm_nt(q_ref[0, h][:, :hd], kc) - slope * dist
        s = jnp.where(valid, s, -jnp.inf)
        m = jnp.max(s, axis=-1, keepdims=True)
        m = jnp.where(m > -jnp.inf, m, 0.0)
        e = jnp.exp(s - m)
        p = e / jnp.maximum(jnp.sum(e, axis=-1, keepdims=True), 1e-30)
        oc_ref[0, :, h * hd:(h + 1) * hd] = _mm(p, vc)
        psum = psum + p
    imp = lax.dot_general(ovt_ref[...], psum, (((1,), (1,)), ((), ())), preferred_element_type=F32, precision=_HI)
    jj = _iota2((nsel, tq), 0)
    cur = (q0 + _iota2((nsel, tq), 1)) // SEL_BLOCK
    forced = (jj == 0) | (jj == cur) | (jj == cur - 1)
    imp = jnp.where(jj <= cur, jnp.where(forced, FORCED_SCORE, imp), -1.0)
    rank = jnp.zeros((nsel, tq), jnp.int32)
    for j2 in range(nsel):
        row = imp[j2:j2 + 1, :]
        before = (row > imp) | ((row == imp) & (jj > j2))
        rank = rank + before.astype(jnp.int32)
    picked = (rank < topk) & (imp >= 0.0)
    bias_t = jnp.where(picked, 0.0, _UNSELECTED)
    pieces = [jnp.zeros((hd, tq), F32), bias_t]
    if nsel < hd:
        pieces.append(jnp.zeros((hd - nsel, tq), F32))
    bias = jnp.concatenate(pieces, axis=0).T
    for h in range(hg):
        qsel_ref[0, h] = (q_ref[0, h].astype(F32) + bias).astype(qsel_ref.dtype)


def _cmpattn(slopes, q_pad, kc, vc, ovt, *, nc, tq=256):
    bg, hg, s_len, _ = q_pad.shape
    tq = min(tq, s_len)
    nsel = s_len // SEL_BLOCK
    nsub = kc.shape[2]
    hd = NSA_HEADDIM
    return pl.pallas_call(
        functools.partial(_cmpattn_kernel, tq=tq, nsel=nsel, nc=nc, topk=min(SEL_TOPK, nsel)),
        grid=(bg, s_len // tq),
        in_specs=[
            pl.BlockSpec(memory_space=pltpu.SMEM),
            pl.BlockSpec((1, hg, tq, 2 * hd), lambda i, j: (i, 0, j, 0)),
            pl.BlockSpec((1, 1, nsub, hd), lambda i, j: (0, i, 0, 0)),
            pl.BlockSpec((1, 1, nsub, hd), lambda i, j: (1, i, 0, 0)),
            pl.BlockSpec((nsel, nsub), lambda i, j: (0, 0)),
        ],
        out_specs=[
            pl.BlockSpec((1, tq, hg * hd), lambda i, j: (i, j, 0)),
            pl.BlockSpec((1, hg, tq, 2 * hd), lambda i, j: (i, 0, j, 0)),
        ],
        out_shape=[
            jax.ShapeDtypeStruct((bg, s_len, hg * hd), F32),
            jax.ShapeDtypeStruct((bg, hg, s_len, 2 * hd), _MXU),
        ],
        compiler_params=pltpu.CompilerParams(
            dimension_semantics=("arbitrary", "arbitrary"), vmem_limit_bytes=_VMEM_LIMIT),
        name="nsa_cmp_attn",
    )(slopes, q_pad, kc, vc, ovt)


def _flash_tiles(q_alls, k_ref, vt_ref, slope_rows, q0, spans, *, tq, tk, hg, window):
    rel = _iota2((tk, tq), 1) - _iota2((tk, tq), 0)
    ngrp = len(q_alls)

    def step(masked, kt, carries):
        k0 = pl.multiple_of(kt * tk, tk)
        if masked:
            d = rel + (q0 - k0)
            ok = d >= 0
            if window is not None:
                ok = ok & (d < window)
            mb = jnp.where(ok, 0.0, _MASKED)
            mb = jnp.concatenate([mb] * hg, axis=1)
        out = []
        for g in range(ngrp):
            m, l, acc = carries[g]
            st = _mm_nt(k_ref[g, pl.ds(k0, tk), :], q_alls[g])
            if masked:
                st = st + mb
            off = slope_rows[g] * k0.astype(F32)
            m_new = jnp.maximum(m, jnp.max(st, axis=0, keepdims=True) + off)
            p = jnp.exp(st - (m_new - off))
            alpha = jnp.exp(m - m_new)
            l = alpha * l + jnp.sum(p, axis=0, keepdims=True)
            acc = alpha * acc + _mm(vt_ref[g, kt], p)
            out.append((m_new, l, acc))
        return tuple(out)

    rows = hg * tq
    init = (jnp.full((1, rows), _MASKED, F32), jnp.zeros((1, rows), F32), jnp.zeros((NSA_HEADDIM, rows), F32))
    carries = (init,) * ngrp
    lo, a, b, hi = spans
    carries = lax.fori_loop(lo, a, functools.partial(step, True), carries)
    carries = lax.fori_loop(a, b, functools.partial(step, False), carries)
    carries = lax.fori_loop(b, hi, functools.partial(step, True), carries)
    return [acc / l for _, l, acc in carries]


def _slope_row(slopes_ref, g, hg, tq):
    return jnp.concatenate([jnp.full((1, tq), slopes_ref[g * hg + h], F32) for h in range(hg)], axis=1)


def _heads_to_lanes(ot, hg, tq):
    pairs = []
    for h in range(0, hg, 2):
        two = jnp.concatenate([ot[:, h * tq:(h + 1) * tq], ot[:, (h + 1) * tq:(h + 2) * tq]], axis=0)
        pairs.append(two.T)
    return jnp.concatenate(pairs, axis=1)


def _selattn_kernel(slopes_ref, q_ref, k_ref, vt_ref, o_ref, *, tq, tk):
    ngrp, hg = NSA_KV_HEADS, NSA_HEADS // NSA_KV_HEADS
    q0 = pl.program_id(1) * tq
    full_hi = (q0 + 1) // tk
    hi = (q0 + tq - 1) // tk + 1
    q_alls = [q_ref[g].reshape(hg * tq, q_ref.shape[-1]) for g in range(ngrp)]
    slope_rows = [_slope_row(slopes_ref, g, hg, tq) for g in range(ngrp)]
    ots = _flash_tiles(q_alls, k_ref, vt_ref, slope_rows, q0, (0, 0, full_hi, hi),
                       tq=tq, tk=tk, hg=hg, window=None)
    for g in range(ngrp):
        o_ref[g] = _heads_to_lanes(ots[g], hg, tq)


def _selattn(slopes, qsel, ksel, vsel_t, *, tq, tk):
    bg, hg, s_len, wid = qsel.shape
    hd, ngrp = NSA_HEADDIM, NSA_KV_HEADS
    return pl.pallas_call(
        functools.partial(_selattn_kernel, tq=tq, tk=tk),
        grid=(bg // ngrp, s_len // tq),
        in_specs=[
            pl.BlockSpec(memory_space=pltpu.SMEM),
            pl.BlockSpec((ngrp, hg, tq, wid), lambda i, j: (i, 0, j, 0)),
            pl.BlockSpec((ngrp, s_len, wid), lambda i, j: (i, 0, 0)),
            pl.BlockSpec((ngrp, s_len // tk, hd, tk), lambda i, j: (i, 0, 0, 0)),
        ],
        out_specs=pl.BlockSpec((ngrp, tq, hg * hd), lambda i, j: (i, j, 0)),
        out_shape=jax.ShapeDtypeStruct((bg, s_len, hg * hd), F32),
        compiler_params=pltpu.CompilerParams(
            dimension_semantics=("arbitrary", "arbitrary"), vmem_limit_bytes=_VMEM_LIMIT),
        name="nsa_sel_attn",
    )(slopes, qsel, ksel, vsel_t)


def _winattn_kernel(slopes_ref, q_ref, k_ref, vt_ref, oc_ref, os_ref, gate_ref, gexp_ref, z0_ref, z1_ref, y_ref,
                    *, tq, tk):
    ngrp, hg = NSA_KV_HEADS, NSA_HEADS // NSA_KV_HEADS
    gw = hg * NSA_HEADDIM
    q0 = pl.program_id(1) * tq
    lo = jnp.maximum(q0 - (WINDOW - 1), 0) // tk
    full_hi = (q0 + 1) // tk
    full_lo = jnp.minimum(jnp.maximum(q0 + tq - WINDOW + tk - 1, 0) // tk, full_hi)
    hi = (q0 + tq - 1) // tk + 1
    q_alls = [q_ref[g].reshape(hg * tq, q_ref.shape[-1]) for g in range(ngrp)]
    slope_rows = [_slope_row(slopes_ref, g, hg, tq) for g in range(ngrp)]
    ots = _flash_tiles(q_alls, k_ref, vt_ref, slope_rows, q0, (lo, full_lo, full_hi, hi),
                       tq=tq, tk=tk, hg=hg, window=WINDOW)
    for g, z_ref in enumerate((z0_ref, z1_ref)):
        o_w = _heads_to_lanes(ots[g], hg, tq)
        gates = jax.nn.sigmoid(gate_ref[g])
        spread = lambda c: _mm_hi(gates, gexp_ref[c])
        o = spread(0) * oc_ref[g] + spread(1) * os_ref[g] + spread(2) * o_w
        y_ref[:, g * gw:(g + 1) * gw] = o * _silu(z_ref[...])


def _winattn(slopes, q_pad, kwin, vwin_t, o_c, o_s, gates, u_all, *, tq, tk):
    bg, hg, s_len, wid = q_pad.shape
    hd, ngrp = NSA_HEADDIM, NSA_KV_HEADS
    nq = s_len // tq
    gw = hg * hd
    zblk = _U_LAYOUT['nsa_z'][0] // gw
    col = jnp.arange(gw)[None, None, :] // hd
    gexp = (jnp.arange(128)[None, :, None] == 3 * col + jnp.arange(3)[:, None, None]).astype(F32)
    return pl.pallas_call(
        functools.partial(_winattn_kernel, tq=tq, tk=tk),
        grid=(bg // ngrp, nq),
        in_specs=[
            pl.BlockSpec(memory_space=pltpu.SMEM),
            pl.BlockSpec((ngrp, hg, tq, wid), lambda i, j: (i, 0, j, 0)),
            pl.BlockSpec((ngrp, s_len, wid), lambda i, j: (i, 0, 0)),
            pl.BlockSpec((ngrp, s_len // tk, hd, tk), lambda i, j: (i, 0, 0, 0)),
            pl.BlockSpec((ngrp, tq, gw), lambda i, j: (i, j, 0)),
            pl.BlockSpec((ngrp, tq, gw), lambda i, j: (i, j, 0)),
            pl.BlockSpec((ngrp, tq, 128), lambda i, j: (i, j, 0)),
            pl.BlockSpec((3, 128, gw), lambda i, j: (0, 0, 0)),
            pl.BlockSpec((tq, gw), lambda i, j: (i * nq + j, zblk)),
            pl.BlockSpec((tq, gw), lambda i, j: (i * nq + j, zblk + 1)),
        ],
        out_specs=pl.BlockSpec((tq, ngrp * gw), lambda i, j: (i * nq + j, 0)),
        out_shape=jax.ShapeDtypeStruct((bg // ngrp * s_len, ngrp * gw), F32),
        compiler_params=pltpu.CompilerParams(
            dimension_semantics=("arbitrary", "arbitrary"), vmem_limit_bytes=_VMEM_LIMIT),
        name="nsa_win_attn",
    )(slopes, q_pad, kwin, vwin_t, o_c, o_s, gates, gexp, u_all, u_all)


def _nsa(u_all, bsz, s_len, cmp_pos_k, cmp_pos_v, w_ck1, w_ck2, w_cv1, w_cv2):
    ng, hg, hd = NSA_KV_HEADS, NSA_HEADS // NSA_KV_HEADS, NSA_HEADDIM
    bg = bsz * ng
    sel_tq, sel_tk = min(_SEL_TQ, s_len), min(_SEL_TK, s_len)
    win_tq, win_tk = min(_WIN_TQ, s_len), min(_WIN_TK, s_len)
    seg = lambda name: lax.slice_in_dim(u_all, _U_LAYOUT[name][0], _U_LAYOUT[name][0] + _U_LAYOUT[name][1], axis=1)
    kvh = lambda name: seg(name).reshape(bsz, s_len, ng, hd).transpose(0, 2, 1, 3).reshape(bg, s_len, hd)
    slopes = 2.0 ** (-8.0 * jnp.arange(1, NSA_HEADS + 1, dtype=F32) / NSA_HEADS)
    q = seg('nsa_q').reshape(bsz, s_len, ng, hg, hd).transpose(0, 2, 3, 1, 4) * (hd ** -0.5)
    slope_lane = jnp.broadcast_to(slopes.reshape(1, ng, hg, 1, 1), (bsz, ng, hg, s_len, 1))
    q_pad = jnp.concatenate([q, slope_lane, jnp.zeros((bsz, ng, hg, s_len, hd - 1), F32)], axis=-1)
    q_pad = q_pad.reshape(bg, hg, s_len, 2 * hd).astype(_MXU)
    nsel = s_len // SEL_BLOCK
    pos = jnp.arange(s_len)
    onehot = (pos[:, None] // SEL_BLOCK == jnp.arange(1, hd)[None, :]).astype(F32)
    kfeat = lambda tk, rest: jnp.broadcast_to(
        jnp.concatenate([(pos % tk).astype(F32)[:, None], rest], axis=-1), (bg, s_len, hd))
    ksel = jnp.concatenate([kvh('nsa_ks'), kfeat(sel_tk, onehot)], axis=-1).astype(_MXU)
    kwin = jnp.concatenate([kvh('nsa_kw'), kfeat(win_tk, jnp.zeros((s_len, hd - 1), F32))], axis=-1).astype(_MXU)
    v_tiles = lambda name, tk: kvh(name).reshape(bg, s_len // tk, tk, hd).transpose(0, 1, 3, 2).astype(_MXU)
    vsel_t = v_tiles('nsa_vs', sel_tk)
    vwin_t = v_tiles('nsa_vw', win_tk)
    nsub = s_len // CMP_STRIDE
    nc = nsub - CMP_BLOCK // CMP_STRIDE + 1
    sub = jnp.stack([kvh('nsa_kc'), kvh('nsa_vc')]).reshape(2, bg, nsub, CMP_STRIDE * hd)
    cmp = _compress(sub, jnp.stack([cmp_pos_k, cmp_pos_v]), jnp.stack([w_ck1, w_cv1]), jnp.stack([w_ck2, w_cv2]))
    cmp_start = jnp.arange(nsub) * CMP_STRIDE
    sel_start = jnp.arange(nsel) * SEL_BLOCK
    ovt = ((cmp_start[None, :] <= sel_start[:, None] + SEL_BLOCK - 1)
           & (cmp_start[None, :] + CMP_BLOCK - 1 >= sel_start[:, None])
           & (jnp.arange(nsub)[None, :] < nc)).astype(F32)
    o_c, qsel = _cmpattn(slopes, q_pad, cmp, cmp, ovt, nc=nc)
    o_s = _selattn(slopes, qsel, ksel, vsel_t, tq=sel_tq, tk=sel_tk)
    gsm = seg('small')[:, SM_GATE:SM_GATE + 3 * NSA_HEADS]
    gates = gsm.reshape(bsz, s_len, ng, 3 * hg).transpose(0, 2, 1, 3).reshape(bg, s_len, 3 * hg)
    gates = jnp.pad(gates, ((0, 0), (0, 0), (0, 128 - 3 * hg)))
    return _winattn(slopes, q_pad, kwin, vwin_t, o_c, o_s, gates, u_all, tq=win_tq, tk=win_tk)


def _merge_kernel(y0_ref, y1_ref, y2_ref, y3_ref, g0_ref, g1_ref, g2_ref, g3_ref, wbr_ref, wout_ref, gain_ref,
                  x_ref, o_ref):
    merged = None
    for n, (y_ref, g_ref) in enumerate(((y0_ref, g0_ref), (y1_ref, g1_ref), (y2_ref, g2_ref), (y3_ref, g3_ref))):
        term = jax.nn.sigmoid(g_ref[...]) * jnp.dot(y_ref[...].astype(_MXU), wbr_ref[n], preferred_element_type=F32)
        merged = term if merged is None else merged + term
    out = jnp.dot(merged.astype(_MXU), wout_ref[...], preferred_element_type=F32)
    out = out * lax.rsqrt(jnp.mean(out * out, axis=-1, keepdims=True) + NORM_EPS) * gain_ref[...]
    o_ref[...] = x_ref[...] + out


def _merge(ys, u_all, w_br, w_out, gain, x2d, *, tm=256):
    t = x2d.shape[0]
    tm = min(tm, t)
    gblk = _U_LAYOUT['merge_gate'][0] // D_MODEL
    yspec = pl.BlockSpec((tm, BR_WIDTH), lambda i: (i, 0))
    gspec = lambda n: pl.BlockSpec((tm, D_MODEL), lambda i: (i, gblk + n))
    return pl.pallas_call(
        _merge_kernel,
        grid=(t // tm,),
        in_specs=[yspec] * 4 + [gspec(n) for n in range(4)] + [
            pl.BlockSpec((N_BRANCH, BR_WIDTH, D_MODEL), lambda i: (0, 0, 0)),
            pl.BlockSpec((D_MODEL, D_MODEL), lambda i: (0, 0)),
            pl.BlockSpec((1, D_MODEL), lambda i: (0, 0)),
            pl.BlockSpec((tm, D_MODEL), lambda i: (i, 0)),
        ],
        out_specs=pl.BlockSpec((tm, D_MODEL), lambda i: (i, 0)),
        out_shape=jax.ShapeDtypeStruct((t, D_MODEL), F32),
        compiler_params=pltpu.CompilerParams(dimension_semantics=("arbitrary",), vmem_limit_bytes=_VMEM_LIMIT),
        name="merge",
    )(*ys, u_all, u_all, u_all, u_all, w_br.astype(_MXU), w_out.astype(_MXU), gain.reshape(1, D_MODEL).astype(F32), x2d)


def _layer(x2d, bsz, s_len, p):
    u_all = _inproj(x2d, p['norm_pre'].astype(F32), _layout_weight(p['w_in']).astype(_MXU))
    ys = (
        _gdn(u_all, bsz, s_len, p['conv_a'], p['a_log_a'], p['dt_bias_a'], p['onorm_a']),
        _gla(u_all, bsz, s_len, p['w_gk'], p['b_gk'], p['onorm_b']),
        _ssd(u_all, bsz, s_len, p['conv_c'], p['conv_bias_c'], p['a_log_c'], p['dt_bias_c'], p['d_skip_c'], p['onorm_c']),
        _nsa(u_all, bsz, s_len, p['cmp_pos_k'], p['cmp_pos_v'], p['w_ck1'], p['w_ck2'], p['w_cv1'], p['w_cv2']),
    )
    return _merge(ys, u_all, p['w_br'], p['w_out'], p['norm_post'], x2d)


def kernel(x, norm_pre, norm_post, w_in, conv_a, a_log_a, dt_bias_a, onorm_a, w_gk, b_gk, onorm_b, conv_c, conv_bias_c, a_log_c, dt_bias_c, d_skip_c, onorm_c, cmp_pos_k, cmp_pos_v, w_ck1, w_ck2, w_cv1, w_cv2, w_br, w_out):
    bsz, s_len, _ = x.shape
    params = dict(norm_pre=norm_pre, norm_post=norm_post, w_in=w_in, conv_a=conv_a, a_log_a=a_log_a,
                  dt_bias_a=dt_bias_a, onorm_a=onorm_a, w_gk=w_gk, b_gk=b_gk, onorm_b=onorm_b, conv_c=conv_c,
                  conv_bias_c=conv_bias_c, a_log_c=a_log_c, dt_bias_c=dt_bias_c, d_skip_c=d_skip_c,
                  onorm_c=onorm_c, cmp_pos_k=cmp_pos_k, cmp_pos_v=cmp_pos_v, w_ck1=w_ck1, w_ck2=w_ck2,
                  w_cv1=w_cv1, w_cv2=w_cv2, w_br=w_br, w_out=w_out)
    x2d = x.reshape(bsz * s_len, D_MODEL)
    for l in range(w_in.shape[0]):
        x2d = _layer(x2d, bsz, s_len, {k: v[l] for k, v in params.items()})
    return x2d.reshape(bsz, s_len, D_MODEL)
```

```python
import functools

import jax
import jax.numpy as jnp
from jax import lax
from jax.experimental import pallas as pl
from jax.experimental.pallas import tpu as pltpu

D_MODEL = 1024
N_BRANCH = 4
BR_WIDTH = 512
CONV_K = 4
NORM_EPS = 1e-6

GDN_HEADS = 4
GDN_DK = 128
GDN_DV = 128
GLA_HEADS = 4
GLA_DK = 64
GLA_DV = 128
GLA_LOWRANK = 16
GLA_GATE_NORMALIZER = 16.0
SSD_HEADS = 8
SSD_HEADDIM = 64
SSD_STATE = 128
SSD_GROUPS = 2
NSA_HEADS = 8
NSA_KV_HEADS = 2
NSA_HEADDIM = 64
CMP_BLOCK = 32
CMP_STRIDE = 16
SEL_BLOCK = 64
SEL_TOPK = 16
WINDOW = 512
FORCED_SCORE = 1e4
CHUNK = 64

F32 = jnp.float32
BF16 = jnp.bfloat16
_MXU = jnp.bfloat16
_VMEM_LIMIT = 56 * 1024 * 1024
_MASKED = -1e30
_UNSELECTED = -1e30
_SEL_TQ, _SEL_TK = 256, 256
_WIN_TQ, _WIN_TK = 256, 256

_U_LAYOUT = {}
_off = 0
for _name, _w in (
    ('gdn_q', 512), ('gdn_k', 512), ('gdn_v', 512), ('gdn_z', 512),
    ('gla_q', 256), ('gla_k', 256), ('gla_v', 512), ('gla_z', 512),
    ('ssd_x', 512), ('ssd_b', 256), ('ssd_c', 256), ('ssd_z', 512),
    ('nsa_q', 512), ('nsa_kc', 128), ('nsa_vc', 128), ('nsa_ks', 128), ('nsa_vs', 128),
    ('nsa_kw', 128), ('nsa_vw', 128), ('nsa_z', 512),
    ('small', 128), ('pad', 128), ('merge_gate', 4096),
):
    _U_LAYOUT[_name] = (_off, _w)
    _off += _w
U_WIDTH = _off
SM_BETA, SM_A, SM_GK, SM_DT, SM_GATE = 0, 4, 8, 24, 32

_IN_SPLITS = (
    ('gdn_q', 512), ('gdn_k', 512), ('gdn_v', 512), ('gdn_beta', 4), ('gdn_a', 4), ('gdn_z', 512),
    ('gla_q', 256), ('gla_k', 256), ('gla_v', 512), ('gla_gk', 16), ('gla_z', 512),
    ('ssd_x', 512), ('ssd_b', 256), ('ssd_c', 256), ('ssd_dt', 8), ('ssd_z', 512),
    ('nsa_q', 512), ('nsa_kc', 128), ('nsa_vc', 128), ('nsa_ks', 128), ('nsa_vs', 128),
    ('nsa_kw', 128), ('nsa_vw', 128), ('nsa_gate', 24), ('nsa_z', 512),
    ('merge_gate', 4096),
)
_SMALL_COL = {'gdn_beta': SM_BETA, 'gdn_a': SM_A, 'gla_gk': SM_GK, 'ssd_dt': SM_DT, 'nsa_gate': SM_GATE}


def _layout_weights(w_in):
    src, start = {}, 0
    for name, width in _IN_SPLITS:
        src[name] = (start, width)
        start += width
    take = lambda name: lax.slice_in_dim(w_in, src[name][0], src[name][0] + src[name][1], axis=w_in.ndim - 1)
    zeros = lambda n: jnp.zeros(w_in.shape[:-1] + (n,), w_in.dtype)
    pieces = []
    for name, (_, width) in _U_LAYOUT.items():
        if name == 'small':
            pos = 0
            for sname, col in sorted(_SMALL_COL.items(), key=lambda kv: kv[1]):
                if col > pos:
                    pieces.append(zeros(col - pos))
                pieces.append(take(sname))
                pos = col + src[sname][1]
            pieces.append(zeros(width - pos))
        elif name == 'pad':
            pieces.append(zeros(width))
        else:
            pieces.append(take(name))
    return jnp.concatenate(pieces, axis=-1)


def _mm(a, b):
    return jnp.dot(a.astype(_MXU), b.astype(_MXU), preferred_element_type=F32)


def _mm_nt(a, b):
    return lax.dot_general(a.astype(_MXU), b.astype(_MXU), (((1,), (1,)), ((), ())),
                           preferred_element_type=F32)


def _mm_tn(a, b):
    return lax.dot_general(a.astype(_MXU), b.astype(_MXU), (((0,), (0,)), ((), ())),
                           preferred_element_type=F32)


def _split_bf16(a):
    hi = a.astype(BF16)
    return hi, (a - hi.astype(F32)).astype(BF16)


def _mm3(a, b):
    ah, al = _split_bf16(a)
    bh, bl = _split_bf16(b)
    d = lambda x, y: jnp.dot(x, y, preferred_element_type=F32)
    return d(ah, bh) + d(ah, bl) + d(al, bh)


def _mm01(a01, b, b_contract=0):
    bh = b.astype(BF16)
    bm, bl = _split_bf16(b - bh.astype(F32))
    a = a01.astype(BF16)
    d = lambda y: lax.dot_general(a, y, (((1,), (b_contract,)), ((), ())), preferred_element_type=F32)
    return d(bh) + d(bm) + d(bl)


def _mm10(a, b01):
    ah = a.astype(BF16)
    am, al = _split_bf16(a - ah.astype(F32))
    b = b01.astype(BF16)
    d = lambda x: jnp.dot(x, b, preferred_element_type=F32)
    return d(ah) + d(am) + d(al)


def _silu(x):
    return x * jax.nn.sigmoid(x)


def _softplus(x):
    return jnp.maximum(x, 0.0) + jnp.log(1.0 + jnp.exp(-jnp.abs(x)))


def _iota2(shape, axis):
    return lax.broadcasted_iota(jnp.int32, shape, axis)


def _causal_conv_silu(xbuf, r0, col, width, w_ref, bias=None, rows=CHUNK):
    xx = xbuf[pl.ds(r0, rows + 8), col:col + width]
    w = w_ref[:, col:col + width]
    acc = xx * w[CONV_K - 1:CONV_K, :]
    for k in range(1, CONV_K):
        acc = acc + pltpu.roll(xx, k, 0) * w[CONV_K - 1 - k:CONV_K - k, :]
    y = acc[8:, :]
    if bias is not None:
        y = y + bias
    return _silu(y)


def _inproj_kernel(x_ref, g_ref, w_ref, o_ref, h_ref):
    @pl.when(pl.program_id(1) == 0)
    def _():
        x = x_ref[...]
        ms = jnp.mean(x * x, axis=-1, keepdims=True)
        h_ref[...] = (x * lax.rsqrt(ms + NORM_EPS) * g_ref[...]).astype(h_ref.dtype)

    o_ref[...] = jnp.dot(h_ref[...], w_ref[...], preferred_element_type=F32)


def _inproj(x2d, gain, w_all, *, tm=1024, tn=1408):
    t = x2d.shape[0]
    tm = min(tm, t)
    return pl.pallas_call(
        _inproj_kernel,
        grid=(t // tm, U_WIDTH // tn),
        in_specs=[
            pl.BlockSpec((tm, D_MODEL), lambda i, j: (i, 0)),
            pl.BlockSpec((1, D_MODEL), lambda i, j: (0, 0)),
            pl.BlockSpec((D_MODEL, tn), lambda i, j: (0, j)),
        ],
        out_specs=pl.BlockSpec((tm, tn), lambda i, j: (i, j)),
        out_shape=jax.ShapeDtypeStruct((t, U_WIDTH), F32),
        scratch_shapes=[pltpu.VMEM((tm, D_MODEL), _MXU)],
        compiler_params=pltpu.CompilerParams(
            dimension_semantics=("arbitrary", "arbitrary"), vmem_limit_bytes=_VMEM_LIMIT),
        name="inproj",
    )(x2d, gain.reshape(1, D_MODEL), w_all)


def _gdn_kernel(qkv_ref, z_ref, sm_ref, cw_ref, alog_ref, dtb_ref, onorm_ref, y_ref, xbuf, state, *, tc):
    c = CHUNK

    @pl.when(pl.program_id(1) == 0)
    def _():
        state[...] = jnp.zeros_like(state)
        xbuf[0:8, :] = jnp.zeros((8, xbuf.shape[1]), F32)

    xbuf[8:8 + tc, :] = qkv_ref[...]

    nch = tc // c
    ri, ci = _iota2((tc, tc), 0), _iota2((tc, tc), 1)
    same = (ri // c) == (ci // c)
    tril = same & (ri >= ci)
    stril = same & (ri > ci)
    tril_f = tril.astype(F32)
    eye_f = (ri == ci).astype(F32)
    neg_a = -jnp.exp(alog_ref[...])
    sm = sm_ref[...]
    beta_all = jax.nn.sigmoid(sm)
    g_all = neg_a * _softplus(sm + dtb_ref[...])
    onorm = onorm_ref[...]

    heads = range(GDN_HEADS)
    qs, ks, vbs, kbs, gcs, decays = [], [], [], [], [], []
    for h in heads:
        q = _causal_conv_silu(xbuf, 0, h * GDN_DK, GDN_DK, cw_ref, rows=tc)
        k = _causal_conv_silu(xbuf, 0, 512 + h * GDN_DK, GDN_DK, cw_ref, rows=tc)
        v = _causal_conv_silu(xbuf, 0, 1024 + h * GDN_DV, GDN_DV, cw_ref, rows=tc)
        q = q * lax.rsqrt(jnp.sum(q * q, axis=-1, keepdims=True) + NORM_EPS) * (GDN_DK ** -0.5)
        k = k * lax.rsqrt(jnp.sum(k * k, axis=-1, keepdims=True) + NORM_EPS)
        beta = jnp.broadcast_to(beta_all[:, SM_BETA + h:SM_BETA + h + 1], (tc, GDN_DK))
        gb = jnp.broadcast_to(g_all[:, SM_A + h:SM_A + h + 1], (tc, GDN_DK))
        gc = _mm01(tril_f, gb)
        gcw = jnp.concatenate([gc] * (tc // GDN_DK), axis=1)
        qs.append(q)
        ks.append(k)
        vbs.append(v * beta)
        kbs.append(k * beta)
        gcs.append(gc)
        decays.append(jnp.exp(jnp.where(tril, gcw - gcw.T, -jnp.inf)))
    lowers = [jnp.where(stril, _mm_nt(kbs[h], ks[h]) * decays[h], 0.0) for h in heads]
    npows = [-lo for lo in lowers]
    invs = [eye_f + n for n in npows]
    for _ in range(5):
        npows = [_mm(n, n) for n in npows]
        invs = [x + _mm(x, n) for x, n in zip(invs, npows)]
    egcs = [jnp.exp(gc) for gc in gcs]
    rhss = [jnp.concatenate([vbs[h], kbs[h] * egcs[h]], axis=1) for h in heads]
    sols = [_mm(invs[h], rhss[h]) for h in heads]
    resids = [rhss[h] - (sols[h] + _mm3(lowers[h], sols[h])) for h in heads]
    sols = [sols[h] + _mm(invs[h], resids[h]) for h in heads]
    attns = [_mm_nt(qs[h], ks[h]) * decays[h] for h in heads]
    qgs = [qs[h] * egcs[h] for h in heads]
    gc_lasts = [jnp.broadcast_to(gc.reshape(nch, c, GDN_DK)[:, c - 1:c, :], (nch, c, GDN_DK)).reshape(tc, GDN_DK)
                for gc in gcs]
    kds = [ks[h] * jnp.exp(gc_lasts[h] - gcs[h]) for h in heads]
    decs = [jnp.exp(g) for g in gc_lasts]
    sts = [state[h] for h in heads]
    v_news = [[] for _ in heads]
    o_inter = [[] for _ in heads]
    for ck in range(nch):
        rows = slice(ck * c, (ck + 1) * c)
        for h in heads:
            v_new = sols[h][rows, :GDN_DV] - _mm(sols[h][rows, GDN_DV:], sts[h])
            o_inter[h].append(_mm(qgs[h][rows], sts[h]))
            sts[h] = sts[h] * decs[h][ck * c:ck * c + 1, :] + _mm_tn(kds[h][rows], v_new)
            v_news[h].append(v_new)
    for h in heads:
        state[h] = sts[h]
        o = jnp.concatenate(o_inter[h], axis=0) + _mm(attns[h], jnp.concatenate(v_news[h], axis=0))
        o = o * lax.rsqrt(jnp.mean(o * o, axis=-1, keepdims=True) + NORM_EPS) * onorm
        y_ref[:, h * GDN_DV:(h + 1) * GDN_DV] = o * _silu(z_ref[:, h * GDN_DV:(h + 1) * GDN_DV])

    xbuf[0:8, :] = qkv_ref[tc - 8:tc, :]


def _small_row(vec, col):
    return jnp.zeros((1, 128), F32).at[0, col:col + vec.shape[0]].set(vec.astype(F32))


def _gdn(u_all, bsz, s_len, conv_w, a_log, dt_bias, onorm, *, tc=256):
    tc = min(tc, s_len)
    ns = s_len // tc
    sm_blk = _U_LAYOUT['small'][0] // 128
    return pl.pallas_call(
        functools.partial(_gdn_kernel, tc=tc),
        grid=(bsz, ns),
        in_specs=[
            pl.BlockSpec((tc, 1536), lambda b, s: (b * ns + s, 0)),
            pl.BlockSpec((tc, 512), lambda b, s: (b * ns + s, _U_LAYOUT['gdn_z'][0] // 512)),
            pl.BlockSpec((tc, 128), lambda b, s: (b * ns + s, sm_blk)),
            pl.BlockSpec((CONV_K, 1536), lambda b, s: (0, 0)),
            pl.BlockSpec((1, 128), lambda b, s: (0, 0)),
            pl.BlockSpec((1, 128), lambda b, s: (0, 0)),
            pl.BlockSpec((1, GDN_DV), lambda b, s: (0, 0)),
        ],
        out_specs=pl.BlockSpec((tc, 512), lambda b, s: (b * ns + s, 0)),
        out_shape=jax.ShapeDtypeStruct((bsz * s_len, 512), F32),
        scratch_shapes=[pltpu.VMEM((tc + 8, 1536), F32), pltpu.VMEM((GDN_HEADS, GDN_DK, GDN_DV), F32)],
        compiler_params=pltpu.CompilerParams(
            dimension_semantics=("arbitrary", "arbitrary"), vmem_limit_bytes=_VMEM_LIMIT),
        name="gdn",
    )(u_all, u_all, u_all, conv_w.astype(F32), _small_row(a_log, SM_A), _small_row(dt_bias, SM_A),
      onorm.reshape(1, GDN_DV).astype(F32))


def _gla_kernel(qk_ref, v_ref, z_ref, sm_ref, wgk_ref, bgk_ref, onorm_ref, y_ref, state_t, *, tc):
    c = CHUNK

    @pl.when(pl.program_id(1) == 0)
    def _():
        state_t[...] = jnp.zeros_like(state_t)

    nch = tc // c
    nk = GLA_HEADS * GLA_DK
    ri, ci = _iota2((tc, tc), 0), _iota2((tc, tc), 1)
    tril = ((ri // c) == (ci // c)) & (ri >= ci)
    onorm = onorm_ref[...]
    heads = range(GLA_HEADS)

    pre = _mm(sm_ref[...], wgk_ref[...]) + bgk_ref[...]
    gk = (jnp.minimum(pre, 0.0) - jnp.log(1.0 + jnp.exp(-jnp.abs(pre)))) * (1.0 / GLA_GATE_NORMALIZER)
    b = _mm01(tril.astype(F32), gk)
    b3 = b.reshape(nch, c, nk)
    at_row = lambda r: jnp.broadcast_to(b3[:, r:r + 1, :], (nch, c, nk)).reshape(tc, nk)
    bref, b_last = at_row(c // 2), at_row(c - 1)
    q = qk_ref[:, 0:nk] * (GLA_DK ** -0.5)
    k = qk_ref[:, nk:2 * nk]
    q_in = q * jnp.exp(b - bref)
    k_in = k * jnp.exp(bref - b)
    qg = q * jnp.exp(b)
    kd = k * jnp.exp(b_last - b)
    dec = jnp.exp(b_last)
    hs = lambda x, h: x[:, h * GLA_DK:(h + 1) * GLA_DK]
    vs = [v_ref[:, h * GLA_DV:(h + 1) * GLA_DV] for h in heads]
    a_intra = [jnp.where(tril, _mm_nt(hs(q_in, h), hs(k_in, h)), 0.0) for h in heads]
    o_intra = [_mm(a_intra[h], vs[h]) for h in heads]
    contrib = [[_mm_tn(vs[h][ck * c:(ck + 1) * c], hs(kd, h)[ck * c:(ck + 1) * c]) for ck in range(nch)]
               for h in heads]
    for h in heads:
        st = state_t[h]
        o_inter = []
        for ck in range(nch):
            o_inter.append(_mm_nt(hs(qg, h)[ck * c:(ck + 1) * c], st))
            st = st * hs(dec, h)[ck * c:ck * c + 1, :] + contrib[h][ck]
        state_t[h] = st
        o = o_intra[h] + jnp.concatenate(o_inter, axis=0)
        o = o * lax.rsqrt(jnp.mean(o * o, axis=-1, keepdims=True) + NORM_EPS) * onorm
        y_ref[:, h * GLA_DV:(h + 1) * GLA_DV] = o * _silu(z_ref[:, h * GLA_DV:(h + 1) * GLA_DV])


def _gla(u_all, bsz, s_len, w_gk, b_gk, onorm, *, tc=256):
    tc = min(tc, s_len)
    ns = s_len // tc
    sm_blk = _U_LAYOUT['small'][0] // 128
    w_pad = jnp.zeros((128, GLA_HEADS * GLA_DK), F32).at[SM_GK:SM_GK + GLA_LOWRANK].set(w_gk.astype(F32))
    row = lambda b, s: b * ns + s
    return pl.pallas_call(
        functools.partial(_gla_kernel, tc=tc),
        grid=(bsz, ns),
        in_specs=[
            pl.BlockSpec((tc, 512), lambda b, s: (row(b, s), _U_LAYOUT['gla_q'][0] // 512)),
            pl.BlockSpec((tc, 512), lambda b, s: (row(b, s), _U_LAYOUT['gla_v'][0] // 512)),
            pl.BlockSpec((tc, 512), lambda b, s: (row(b, s), _U_LAYOUT['gla_z'][0] // 512)),
            pl.BlockSpec((tc, 128), lambda b, s: (row(b, s), sm_blk)),
            pl.BlockSpec((128, 256), lambda b, s: (0, 0)),
            pl.BlockSpec((1, 256), lambda b, s: (0, 0)),
            pl.BlockSpec((1, GLA_DV), lambda b, s: (0, 0)),
        ],
        out_specs=pl.BlockSpec((tc, 512), lambda b, s: (row(b, s), 0)),
        out_shape=jax.ShapeDtypeStruct((bsz * s_len, 512), F32),
        scratch_shapes=[pltpu.VMEM((GLA_HEADS, GLA_DV, GLA_DK), F32)],
        compiler_params=pltpu.CompilerParams(
            dimension_semantics=("arbitrary", "arbitrary"), vmem_limit_bytes=_VMEM_LIMIT),
        name="gla",
    )(u_all, u_all, u_all, u_all, w_pad, b_gk.reshape(1, -1).astype(F32), onorm.reshape(1, GLA_DV).astype(F32))


def _ssd_kernel(x_ref, bc_ref, z_ref, sm_ref, cw_ref, cb_ref, alog_ref, dtb_ref, dskip_ref, onorm_ref,
                y_ref, xbuf, state, *, tc):
    c = CHUNK
    hg = SSD_HEADS // SSD_GROUPS

    @pl.when(pl.program_id(1) == 0)
    def _():
        state[...] = jnp.zeros_like(state)
        xbuf[0:8, :] = jnp.zeros((8, xbuf.shape[1]), F32)

    xbuf[8:8 + tc, 0:512] = x_ref[...]
    xbuf[8:8 + tc, 512:1024] = bc_ref[...]

    nch = tc // c
    inner = SSD_HEADS * SSD_HEADDIM
    ri, ci = _iota2((tc, tc), 0), _iota2((tc, tc), 1)
    tril = ((ri // c) == (ci // c)) & (ri >= ci)
    neg_a = -jnp.exp(alog_ref[...])
    cbias = cb_ref[...]
    heads = range(SSD_HEADS)

    dt_all = _softplus(sm_ref[...] + dtb_ref[...])
    acs_all = _mm01(tril.astype(F32), dt_all * neg_a)
    acs_last_all = jnp.broadcast_to(acs_all.reshape(nch, c, 128)[:, c - 1:c, :], (nch, c, 128)).reshape(tc, 128)
    spread_m = (_iota2((128, inner), 0) == SM_DT + _iota2((128, inner), 1) // SSD_HEADDIM).astype(F32)
    spread = lambda cols: _mm10(cols, spread_m)
    xs = _causal_conv_silu(xbuf, 0, 0, inner, cw_ref, cbias[:, 0:inner], rows=tc)
    xdt = xs * spread(dt_all)
    xdec = xdt * spread(jnp.exp(acs_last_all - acs_all))
    bms, cms, cbs = [], [], []
    for g in range(SSD_GROUPS):
        lo = inner + g * SSD_STATE
        bms.append(_causal_conv_silu(xbuf, 0, lo, SSD_STATE, cw_ref, cbias[:, lo:lo + SSD_STATE], rows=tc))
        lo = inner + SSD_GROUPS * SSD_STATE + g * SSD_STATE
        cms.append(_causal_conv_silu(xbuf, 0, lo, SSD_STATE, cw_ref, cbias[:, lo:lo + SSD_STATE], rows=tc))
        cbs.append(_mm_nt(cms[g], bms[g]))
    hl = lambda x, hh: x[:, hh * SSD_HEADDIM:(hh + 1) * SSD_HEADDIM]
    lmats = []
    for hh in heads:
        acs_b = jnp.broadcast_to(acs_all[:, SM_DT + hh:SM_DT + hh + 1], (tc, tc))
        lmats.append(jnp.exp(jnp.where(tril, acs_b - acs_b.T, -jnp.inf)))
    y_diag = [_mm(cbs[hh // hg] * lmats[hh], hl(xdt, hh)) for hh in heads]
    contrib = [[_mm_tn(hl(xdec, hh)[ck * c:(ck + 1) * c], bms[hh // hg][ck * c:(ck + 1) * c]) for ck in range(nch)]
               for hh in heads]
    cdec = jnp.exp(acs_last_all)
    y_off = []
    for hh in heads:
        st = state[hh]
        parts = []
        for ck in range(nch):
            parts.append(_mm_nt(cms[hh // hg][ck * c:(ck + 1) * c], st))
            st = st * cdec[ck * c:ck * c + 1, SM_DT + hh:SM_DT + hh + 1] + contrib[hh][ck]
        state[hh] = st
        y_off.append(jnp.concatenate(parts, axis=0))
    y = (jnp.concatenate(y_diag, axis=1) + jnp.concatenate(y_off, axis=1) * spread(jnp.exp(acs_all))
         + xs * dskip_ref[...])
    y = y * _silu(z_ref[...])
    y_ref[...] = y * lax.rsqrt(jnp.mean(y * y, axis=-1, keepdims=True) + NORM_EPS) * onorm_ref[...]
    xbuf[0:8, 0:512] = x_ref[tc - 8:tc, :]
    xbuf[0:8, 512:1024] = bc_ref[tc - 8:tc, :]


def _ssd(u_all, bsz, s_len, conv_w, conv_b, a_log, dt_bias, d_skip, onorm, *, tc=256):
    tc = min(tc, s_len)
    ns = s_len // tc
    sm_blk = _U_LAYOUT['small'][0] // 128
    row = lambda b, s: b * ns + s
    const = lambda b, s: (0, 0)
    return pl.pallas_call(
        functools.partial(_ssd_kernel, tc=tc),
        grid=(bsz, ns),
        in_specs=[
            pl.BlockSpec((tc, 512), lambda b, s: (row(b, s), _U_LAYOUT['ssd_x'][0] // 512)),
            pl.BlockSpec((tc, 512), lambda b, s: (row(b, s), _U_LAYOUT['ssd_b'][0] // 512)),
            pl.BlockSpec((tc, 512), lambda b, s: (row(b, s), _U_LAYOUT['ssd_z'][0] // 512)),
            pl.BlockSpec((tc, 128), lambda b, s: (row(b, s), sm_blk)),
            pl.BlockSpec((CONV_K, 1024), const),
            pl.BlockSpec((1, 1024), const),
            pl.BlockSpec((1, 128), const),
            pl.BlockSpec((1, 128), const),
            pl.BlockSpec((1, 512), const),
            pl.BlockSpec((1, 512), const),
        ],
        out_specs=pl.BlockSpec((tc, 512), lambda b, s: (row(b, s), 0)),
        out_shape=jax.ShapeDtypeStruct((bsz * s_len, 512), F32),
        scratch_shapes=[pltpu.VMEM((tc + 8, 1024), F32), pltpu.VMEM((SSD_HEADS, SSD_HEADDIM, SSD_STATE), F32)],
        compiler_params=pltpu.CompilerParams(
            dimension_semantics=("arbitrary", "arbitrary"), vmem_limit_bytes=_VMEM_LIMIT),
        name="ssd",
    )(u_all, u_all, u_all, u_all, conv_w.astype(F32), conv_b.reshape(1, -1).astype(F32),
      _small_row(a_log, SM_DT), _small_row(dt_bias, SM_DT),
      jnp.repeat(d_skip.astype(F32), SSD_HEADDIM).reshape(1, 512), onorm.reshape(1, 512).astype(F32))


def _compress_kernel(a_ref, pa_ref, pb_ref, w1a_ref, w1b_ref, w2_ref, o_ref):
    a = a_ref[0, 0]
    nsub = a.shape[0]
    e = _mm(a + pa_ref[0], w1a_ref[0])
    f = _mm(a + pb_ref[0], w1b_ref[0])
    pre = e + pltpu.roll(f, nsub - 1, 0)
    o_ref[0, 0] = _mm(_silu(pre), w2_ref[0])


def _compress(sub, pos, w1, w2):
    _, bg, nsub, wid = sub.shape
    hd = NSA_HEADDIM
    pa = pos[:, :CMP_STRIDE].reshape(2, 1, wid).astype(F32)
    pb = pos[:, CMP_STRIDE:].reshape(2, 1, wid).astype(F32)
    w1a = w1[:, :wid].astype(_MXU)
    w1b = w1[:, wid:].astype(_MXU)
    sel = lambda kv, i: (kv, 0, 0)
    return pl.pallas_call(
        _compress_kernel,
        grid=(2, bg),
        in_specs=[
            pl.BlockSpec((1, 1, nsub, wid), lambda kv, i: (kv, i, 0, 0)),
            pl.BlockSpec((1, 1, wid), sel),
            pl.BlockSpec((1, 1, wid), sel),
            pl.BlockSpec((1, wid, hd), sel),
            pl.BlockSpec((1, wid, hd), sel),
            pl.BlockSpec((1, hd, hd), sel),
        ],
        out_specs=pl.BlockSpec((1, 1, nsub, hd), lambda kv, i: (kv, i, 0, 0)),
        out_shape=jax.ShapeDtypeStruct((2, bg, nsub, hd), F32),
        compiler_params=pltpu.CompilerParams(
            dimension_semantics=("arbitrary", "arbitrary"), vmem_limit_bytes=_VMEM_LIMIT),
        name="nsa_compress",
    )(sub, pa, pb, w1a, w1b, w2.astype(_MXU))


def _cmpattn_kernel(slopes_ref, q_ref, kc_ref, vc_ref, ovt_ref, oc_ref, qsel_ref, *, tq, nsel, nc, topk):
    hd = NSA_HEADDIM
    hg = NSA_HEADS // NSA_KV_HEADS
    g = pl.program_id(0) % NSA_KV_HEADS
    q0 = pl.program_id(1) * tq
    kc = kc_ref[0, 0]
    vc = vc_ref[0, 0]
    nsub = kc.shape[0]
    t = q0 + _iota2((tq, nsub), 0)
    ccol = _iota2((tq, nsub), 1)
    cpos = ccol * CMP_STRIDE + (CMP_BLOCK - 1)
    valid = (cpos <= t) & (ccol < nc)
    dist = (t - cpos).astype(F32)
    psum = jnp.zeros((tq, nsub), F32)
    for h in range(hg):
        slope = slopes_ref[g * hg + h]
        s = _mm_nt(q_ref[0, h][:, :hd], kc) - slope * dist
        s = jnp.where(valid, s, -jnp.inf)
        m = jnp.max(s, axis=-1, keepdims=True)
        m = jnp.where(m > -jnp.inf, m, 0.0)
        e = jnp.exp(s - m)
        p = e / jnp.maximum(jnp.sum(e, axis=-1, keepdims=True), 1e-30)
        oc_ref[0, :, h * hd:(h + 1) * hd] = _mm(p, vc)
        psum = psum + p
    imp = _mm01(ovt_ref[...], psum, b_contract=1)
    jj = _iota2((nsel, tq), 0)
    cur = (q0 + _iota2((nsel, tq), 1)) // SEL_BLOCK
    forced = (jj == 0) | (jj == cur) | (jj == cur - 1)
    imp = jnp.where(jj <= cur, jnp.where(forced, FORCED_SCORE, imp), -1.0)
    rank = jnp.zeros((nsel, tq), jnp.int32)
    for j2 in range(nsel):
        row = imp[j2:j2 + 1, :]
        before = (row > imp) | ((row == imp) & (jj > j2))
        rank = rank + before.astype(jnp.int32)
    picked = (rank < topk) & (imp >= 0.0)
    bias_t = jnp.where(picked, 0.0, _UNSELECTED)
    pieces = [jnp.zeros((hd, tq), F32), bias_t]
    if nsel < hd:
        pieces.append(jnp.zeros((hd - nsel, tq), F32))
    bias = jnp.concatenate(pieces, axis=0).T
    for h in range(hg):
        qsel_ref[0, h] = (q_ref[0, h].astype(F32) + bias).astype(qsel_ref.dtype)


def _cmpattn(slopes, q_pad, kc, vc, ovt, *, nc, tq=256):
    bg, hg, s_len, _ = q_pad.shape
    tq = min(tq, s_len)
    nsel = s_len // SEL_BLOCK
    nsub = kc.shape[2]
    hd = NSA_HEADDIM
    return pl.pallas_call(
        functools.partial(_cmpattn_kernel, tq=tq, nsel=nsel, nc=nc, topk=min(SEL_TOPK, nsel)),
        grid=(bg, s_len // tq),
        in_specs=[
            pl.BlockSpec(memory_space=pltpu.SMEM),
            pl.BlockSpec((1, hg, tq, 2 * hd), lambda i, j: (i, 0, j, 0)),
            pl.BlockSpec((1, 1, nsub, hd), lambda i, j: (0, i, 0, 0)),
            pl.BlockSpec((1, 1, nsub, hd), lambda i, j: (1, i, 0, 0)),
            pl.BlockSpec((nsel, nsub), lambda i, j: (0, 0)),
        ],
        out_specs=[
            pl.BlockSpec((1, tq, hg * hd), lambda i, j: (i, j, 0)),
            pl.BlockSpec((1, hg, tq, 2 * hd), lambda i, j: (i, 0, j, 0)),
        ],
        out_shape=[
            jax.ShapeDtypeStruct((bg, s_len, hg * hd), F32),
            jax.ShapeDtypeStruct((bg, hg, s_len, 2 * hd), _MXU),
        ],
        compiler_params=pltpu.CompilerParams(
            dimension_semantics=("arbitrary", "arbitrary"), vmem_limit_bytes=_VMEM_LIMIT),
        name="nsa_cmp_attn",
    )(slopes, q_pad, kc, vc, ovt)


def _flash_tiles(q_alls, k_ref, vt_ref, slope_rows, q0, spans, *, tq, tk, hg, window):
    rel = _iota2((tk, tq), 1) - _iota2((tk, tq), 0)
    ngrp = len(q_alls)

    def step(masked, kt, carries):
        k0 = pl.multiple_of(kt * tk, tk)
        if masked:
            d = rel + (q0 - k0)
            ok = d >= 0
            if window is not None:
                ok = ok & (d < window)
            mb = jnp.where(ok, 0.0, _MASKED)
            mb = jnp.concatenate([mb] * hg, axis=1)
        grps = range(ngrp)
        sts = [_mm_nt(k_ref[g, pl.ds(k0, tk), :], q_alls[g]) for g in grps]
        if masked:
            sts = [st + mb for st in sts]
        offs = [slope_rows[g] * k0.astype(F32) for g in grps]
        m_news = [jnp.maximum(carries[g][0], jnp.max(sts[g], axis=0, keepdims=True) + offs[g]) for g in grps]
        ps = [jnp.exp(sts[g] - (m_news[g] - offs[g])) for g in grps]
        alphas = [jnp.exp(carries[g][0] - m_news[g]) for g in grps]
        ls = [alphas[g] * carries[g][1] + jnp.sum(ps[g], axis=0, keepdims=True) for g in grps]
        accs = [alphas[g] * carries[g][2] + _mm(vt_ref[g, kt], ps[g]) for g in grps]
        return tuple((m_news[g], ls[g], accs[g]) for g in grps)

    rows = hg * tq
    init = (jnp.full((1, rows), _MASKED, F32), jnp.zeros((1, rows), F32), jnp.zeros((NSA_HEADDIM, rows), F32))
    carries = (init,) * ngrp
    lo, a, b, hi = spans
    carries = lax.fori_loop(lo, a, functools.partial(step, True), carries)
    carries = lax.fori_loop(a, b, functools.partial(step, False), carries)
    carries = lax.fori_loop(b, hi, functools.partial(step, True), carries)
    return [acc / l for _, l, acc in carries]


def _slope_row(slopes_ref, g, hg, tq):
    return jnp.concatenate([jnp.full((1, tq), slopes_ref[g * hg + h], F32) for h in range(hg)], axis=1)


def _heads_to_lanes(ot, hg, tq):
    pairs = []
    for h in range(0, hg, 2):
        two = jnp.concatenate([ot[:, h * tq:(h + 1) * tq], ot[:, (h + 1) * tq:(h + 2) * tq]], axis=0)
        pairs.append(two.T)
    return jnp.concatenate(pairs, axis=1)


def _selattn_kernel(slopes_ref, q_ref, k_ref, vt_ref, o_ref, *, tq, tk):
    ngrp, hg = NSA_KV_HEADS, NSA_HEADS // NSA_KV_HEADS
    q0 = pl.program_id(1) * tq
    full_hi = (q0 + 1) // tk
    hi = (q0 + tq - 1) // tk + 1
    q_alls = [q_ref[g].reshape(hg * tq, q_ref.shape[-1]) for g in range(ngrp)]
    slope_rows = [_slope_row(slopes_ref, g, hg, tq) for g in range(ngrp)]
    ots = _flash_tiles(q_alls, k_ref, vt_ref, slope_rows, q0, (0, 0, full_hi, hi),
                       tq=tq, tk=tk, hg=hg, window=None)
    for g in range(ngrp):
        o_ref[g] = _heads_to_lanes(ots[g], hg, tq)


def _selattn(slopes, qsel, ksel, vsel_t, *, tq, tk):
    bg, hg, s_len, wid = qsel.shape
    hd, ngrp = NSA_HEADDIM, NSA_KV_HEADS
    return pl.pallas_call(
        functools.partial(_selattn_kernel, tq=tq, tk=tk),
        grid=(bg // ngrp, s_len // tq),
        in_specs=[
            pl.BlockSpec(memory_space=pltpu.SMEM),
            pl.BlockSpec((ngrp, hg, tq, wid), lambda i, j: (i, 0, j, 0)),
            pl.BlockSpec((ngrp, s_len, wid), lambda i, j: (i, 0, 0)),
            pl.BlockSpec((ngrp, s_len // tk, hd, tk), lambda i, j: (i, 0, 0, 0)),
        ],
        out_specs=pl.BlockSpec((ngrp, tq, hg * hd), lambda i, j: (i, j, 0)),
        out_shape=jax.ShapeDtypeStruct((bg, s_len, hg * hd), F32),
        compiler_params=pltpu.CompilerParams(
            dimension_semantics=("arbitrary", "arbitrary"), vmem_limit_bytes=_VMEM_LIMIT),
        name="nsa_sel_attn",
    )(slopes, qsel, ksel, vsel_t)


def _winattn_kernel(slopes_ref, q_ref, k_ref, vt_ref, oc_ref, os_ref, gate_ref, gexp_ref, z0_ref, z1_ref, y_ref,
                    *, tq, tk):
    ngrp, hg = NSA_KV_HEADS, NSA_HEADS // NSA_KV_HEADS
    gw = hg * NSA_HEADDIM
    q0 = pl.program_id(1) * tq
    lo = jnp.maximum(q0 - (WINDOW - 1), 0) // tk
    full_hi = (q0 + 1) // tk
    full_lo = jnp.minimum(jnp.maximum(q0 + tq - WINDOW + tk - 1, 0) // tk, full_hi)
    hi = (q0 + tq - 1) // tk + 1
    q_alls = [q_ref[g].reshape(hg * tq, q_ref.shape[-1]) for g in range(ngrp)]
    slope_rows = [_slope_row(slopes_ref, g, hg, tq) for g in range(ngrp)]
    ots = _flash_tiles(q_alls, k_ref, vt_ref, slope_rows, q0, (lo, full_lo, full_hi, hi),
                       tq=tq, tk=tk, hg=hg, window=WINDOW)
    for g, z_ref in enumerate((z0_ref, z1_ref)):
        o_w = _heads_to_lanes(ots[g], hg, tq)
        gates = jax.nn.sigmoid(gate_ref[g])
        spread = lambda c: _mm10(gates, gexp_ref[c])
        o = spread(0) * oc_ref[g] + spread(1) * os_ref[g] + spread(2) * o_w
        y_ref[:, g * gw:(g + 1) * gw] = o * _silu(z_ref[...])


def _winattn(slopes, q_pad, kwin, vwin_t, o_c, o_s, gates, u_all, *, tq, tk):
    bg, hg, s_len, wid = q_pad.shape
    hd, ngrp = NSA_HEADDIM, NSA_KV_HEADS
    nq = s_len // tq
    gw = hg * hd
    zblk = _U_LAYOUT['nsa_z'][0] // gw
    col = jnp.arange(gw)[None, None, :] // hd
    gexp = (jnp.arange(128)[None, :, None] == 3 * col + jnp.arange(3)[:, None, None]).astype(F32)
    return pl.pallas_call(
        functools.partial(_winattn_kernel, tq=tq, tk=tk),
        grid=(bg // ngrp, nq),
        in_specs=[
            pl.BlockSpec(memory_space=pltpu.SMEM),
            pl.BlockSpec((ngrp, hg, tq, wid), lambda i, j: (i, 0, j, 0)),
            pl.BlockSpec((ngrp, s_len, wid), lambda i, j: (i, 0, 0)),
            pl.BlockSpec((ngrp, s_len // tk, hd, tk), lambda i, j: (i, 0, 0, 0)),
            pl.BlockSpec((ngrp, tq, gw), lambda i, j: (i, j, 0)),
            pl.BlockSpec((ngrp, tq, gw), lambda i, j: (i, j, 0)),
            pl.BlockSpec((ngrp, tq, 128), lambda i, j: (i, j, 0)),
            pl.BlockSpec((3, 128, gw), lambda i, j: (0, 0, 0)),
            pl.BlockSpec((tq, gw), lambda i, j: (i * nq + j, zblk)),
            pl.BlockSpec((tq, gw), lambda i, j: (i * nq + j, zblk + 1)),
        ],
        out_specs=pl.BlockSpec((tq, ngrp * gw), lambda i, j: (i * nq + j, 0)),
        out_shape=jax.ShapeDtypeStruct((bg // ngrp * s_len, ngrp * gw), F32),
        compiler_params=pltpu.CompilerParams(
            dimension_semantics=("arbitrary", "arbitrary"), vmem_limit_bytes=_VMEM_LIMIT),
        name="nsa_win_attn",
    )(slopes, q_pad, kwin, vwin_t, o_c, o_s, gates, gexp, u_all, u_all)


def _nsa(u_all, bsz, s_len, cmp_pos_k, cmp_pos_v, w_ck1, w_ck2, w_cv1, w_cv2):
    ng, hg, hd = NSA_KV_HEADS, NSA_HEADS // NSA_KV_HEADS, NSA_HEADDIM
    bg = bsz * ng
    sel_tq, sel_tk = min(_SEL_TQ, s_len), min(_SEL_TK, s_len)
    win_tq, win_tk = min(_WIN_TQ, s_len), min(_WIN_TK, s_len)
    seg = lambda name: lax.slice_in_dim(u_all, _U_LAYOUT[name][0], _U_LAYOUT[name][0] + _U_LAYOUT[name][1], axis=1)
    kvh = lambda name: seg(name).reshape(bsz, s_len, ng, hd).transpose(0, 2, 1, 3).reshape(bg, s_len, hd)
    slopes = 2.0 ** (-8.0 * jnp.arange(1, NSA_HEADS + 1, dtype=F32) / NSA_HEADS)
    q = seg('nsa_q').reshape(bsz, s_len, ng, hg, hd).transpose(0, 2, 3, 1, 4) * (hd ** -0.5)
    slope_lane = jnp.broadcast_to(slopes.reshape(1, ng, hg, 1, 1), (bsz, ng, hg, s_len, 1))
    q_pad = jnp.concatenate([q, slope_lane, jnp.zeros((bsz, ng, hg, s_len, hd - 1), F32)], axis=-1)
    q_pad = q_pad.reshape(bg, hg, s_len, 2 * hd).astype(_MXU)
    nsel = s_len // SEL_BLOCK
    pos = jnp.arange(s_len)
    onehot = (pos[:, None] // SEL_BLOCK == jnp.arange(1, hd)[None, :]).astype(F32)
    kfeat = lambda tk, rest: jnp.broadcast_to(
        jnp.concatenate([(pos % tk).astype(F32)[:, None], rest], axis=-1), (bg, s_len, hd))
    ksel = jnp.concatenate([kvh('nsa_ks'), kfeat(sel_tk, onehot)], axis=-1).astype(_MXU)
    kwin = jnp.concatenate([kvh('nsa_kw'), kfeat(win_tk, jnp.zeros((s_len, hd - 1), F32))], axis=-1).astype(_MXU)
    v_tiles = lambda name, tk: kvh(name).reshape(bg, s_len // tk, tk, hd).transpose(0, 1, 3, 2).astype(_MXU)
    vsel_t = v_tiles('nsa_vs', sel_tk)
    vwin_t = v_tiles('nsa_vw', win_tk)
    nsub = s_len // CMP_STRIDE
    nc = nsub - CMP_BLOCK // CMP_STRIDE + 1
    sub = jnp.stack([kvh('nsa_kc'), kvh('nsa_vc')]).reshape(2, bg, nsub, CMP_STRIDE * hd)
    cmp = _compress(sub, jnp.stack([cmp_pos_k, cmp_pos_v]), jnp.stack([w_ck1, w_cv1]), jnp.stack([w_ck2, w_cv2]))
    cmp_start = jnp.arange(nsub) * CMP_STRIDE
    sel_start = jnp.arange(nsel) * SEL_BLOCK
    ovt = ((cmp_start[None, :] <= sel_start[:, None] + SEL_BLOCK - 1)
           & (cmp_start[None, :] + CMP_BLOCK - 1 >= sel_start[:, None])
           & (jnp.arange(nsub)[None, :] < nc)).astype(F32)
    o_c, qsel = _cmpattn(slopes, q_pad, cmp, cmp, ovt, nc=nc)
    o_s = _selattn(slopes, qsel, ksel, vsel_t, tq=sel_tq, tk=sel_tk)
    gsm = seg('small')[:, SM_GATE:SM_GATE + 3 * NSA_HEADS]
    gates = gsm.reshape(bsz, s_len, ng, 3 * hg).transpose(0, 2, 1, 3).reshape(bg, s_len, 3 * hg)
    gates = jnp.pad(gates, ((0, 0), (0, 0), (0, 128 - 3 * hg)))
    return _winattn(slopes, q_pad, kwin, vwin_t, o_c, o_s, gates, u_all, tq=win_tq, tk=win_tk)


def _merge_kernel(y0_ref, y1_ref, y2_ref, y3_ref, g0_ref, g1_ref, g2_ref, g3_ref, wbr_ref, wout_ref, gain_ref,
                  x_ref, o_ref):
    merged = None
    for n, (y_ref, g_ref) in enumerate(((y0_ref, g0_ref), (y1_ref, g1_ref), (y2_ref, g2_ref), (y3_ref, g3_ref))):
        term = jax.nn.sigmoid(g_ref[...]) * jnp.dot(y_ref[...].astype(_MXU), wbr_ref[n], preferred_element_type=F32)
        merged = term if merged is None else merged + term
    out = jnp.dot(merged.astype(_MXU), wout_ref[...], preferred_element_type=F32)
    out = out * lax.rsqrt(jnp.mean(out * out, axis=-1, keepdims=True) + NORM_EPS) * gain_ref[...]
    o_ref[...] = x_ref[...] + out


def _merge(ys, u_all, w_br, w_out, gain, x2d, *, tm=256):
    t = x2d.shape[0]
    tm = min(tm, t)
    gblk = _U_LAYOUT['merge_gate'][0] // D_MODEL
    yspec = pl.BlockSpec((tm, BR_WIDTH), lambda i: (i, 0))
    gspec = lambda n: pl.BlockSpec((tm, D_MODEL), lambda i: (i, gblk + n))
    return pl.pallas_call(
        _merge_kernel,
        grid=(t // tm,),
        in_specs=[yspec] * 4 + [gspec(n) for n in range(4)] + [
            pl.BlockSpec((N_BRANCH, BR_WIDTH, D_MODEL), lambda i: (0, 0, 0)),
            pl.BlockSpec((D_MODEL, D_MODEL), lambda i: (0, 0)),
            pl.BlockSpec((1, D_MODEL), lambda i: (0, 0)),
            pl.BlockSpec((tm, D_MODEL), lambda i: (i, 0)),
        ],
        out_specs=pl.BlockSpec((tm, D_MODEL), lambda i: (i, 0)),
        out_shape=jax.ShapeDtypeStruct((t, D_MODEL), F32),
        compiler_params=pltpu.CompilerParams(dimension_semantics=("arbitrary",), vmem_limit_bytes=_VMEM_LIMIT),
        name="merge",
    )(*ys, u_all, u_all, u_all, u_all, w_br.astype(_MXU), w_out.astype(_MXU), gain.reshape(1, D_MODEL).astype(F32), x2d)


def _layer(x2d, bsz, s_len, p):
    u_all = _inproj(x2d, p['norm_pre'].astype(F32), p['w_all'])
    ys = (
        _gdn(u_all, bsz, s_len, p['conv_a'], p['a_log_a'], p['dt_bias_a'], p['onorm_a']),
        _gla(u_all, bsz, s_len, p['w_gk'], p['b_gk'], p['onorm_b']),
        _ssd(u_all, bsz, s_len, p['conv_c'], p['conv_bias_c'], p['a_log_c'], p['dt_bias_c'], p['d_skip_c'], p['onorm_c']),
        _nsa(u_all, bsz, s_len, p['cmp_pos_k'], p['cmp_pos_v'], p['w_ck1'], p['w_ck2'], p['w_cv1'], p['w_cv2']),
    )
    return _merge(ys, u_all, p['w_br'], p['w_out'], p['norm_post'], x2d)


def kernel(x, norm_pre, norm_post, w_in, conv_a, a_log_a, dt_bias_a, onorm_a, w_gk, b_gk, onorm_b, conv_c, conv_bias_c, a_log_c, dt_bias_c, d_skip_c, onorm_c, cmp_pos_k, cmp_pos_v, w_ck1, w_ck2, w_cv1, w_cv2, w_br, w_out):
    bsz, s_len, _ = x.shape
    params = dict(norm_pre=norm_pre, norm_post=norm_post, w_all=_layout_weights(w_in).astype(_MXU),
                  conv_a=conv_a, a_log_a=a_log_a,
                  dt_bias_a=dt_bias_a, onorm_a=onorm_a, w_gk=w_gk, b_gk=b_gk, onorm_b=onorm_b, conv_c=conv_c,
                  conv_bias_c=conv_bias_c, a_log_c=a_log_c, dt_bias_c=dt_bias_c, d_skip_c=d_skip_c,
                  onorm_c=onorm_c, cmp_pos_k=cmp_pos_k, cmp_pos_v=cmp_pos_v, w_ck1=w_ck1, w_ck2=w_ck2,
                  w_cv1=w_cv1, w_cv2=w_cv2, w_br=w_br, w_out=w_out)
    x2d = x.reshape(bsz * s_len, D_MODEL)
    for l in range(w_in.shape[0]):
        x2d = _layer(x2d, bsz, s_len, {k: v[l] for k, v in params.items()})
    return x2d.reshape(bsz, s_len, D_MODEL)
```

```python
import functools

import jax
import jax.numpy as jnp
from jax import lax
from jax.experimental import pallas as pl
from jax.experimental.pallas import tpu as pltpu

D_MODEL = 1024
N_BRANCH = 4
BR_WIDTH = 512
CONV_K = 4
NORM_EPS = 1e-6

GDN_HEADS = 4
GDN_DK = 128
GDN_DV = 128
GLA_HEADS = 4
GLA_DK = 64
GLA_DV = 128
GLA_LOWRANK = 16
GLA_GATE_NORMALIZER = 16.0
SSD_HEADS = 8
SSD_HEADDIM = 64
SSD_STATE = 128
SSD_GROUPS = 2
NSA_HEADS = 8
NSA_KV_HEADS = 2
NSA_HEADDIM = 64
CMP_BLOCK = 32
CMP_STRIDE = 16
SEL_BLOCK = 64
SEL_TOPK = 16
WINDOW = 512
FORCED_SCORE = 1e4
CHUNK = 64

F32 = jnp.float32
BF16 = jnp.bfloat16
_MXU = jnp.bfloat16
_VMEM_LIMIT = 56 * 1024 * 1024
_MASKED = -1e30
_UNSELECTED = -1e30
_NSA_TQ, _NSA_TK = 256, 256

_U_LAYOUT = {}
_off = 0
for _name, _w in (
    ('gdn_q', 512), ('gdn_k', 512), ('gdn_v', 512), ('gdn_z', 512),
    ('gla_q', 256), ('gla_k', 256), ('gla_v', 512), ('gla_z', 512),
    ('ssd_x', 512), ('ssd_b', 256), ('ssd_c', 256), ('ssd_z', 512),
    ('nsa_q', 512), ('nsa_kc', 128), ('nsa_vc', 128), ('nsa_ks', 128), ('nsa_vs', 128),
    ('nsa_kw', 128), ('nsa_vw', 128), ('nsa_z', 512),
    ('small', 128), ('pad', 128), ('merge_gate', 4096),
):
    _U_LAYOUT[_name] = (_off, _w)
    _off += _w
U_WIDTH = _off
SM_BETA, SM_A, SM_GK, SM_DT, SM_GATE = 0, 4, 8, 24, 32

_IN_SPLITS = (
    ('gdn_q', 512), ('gdn_k', 512), ('gdn_v', 512), ('gdn_beta', 4), ('gdn_a', 4), ('gdn_z', 512),
    ('gla_q', 256), ('gla_k', 256), ('gla_v', 512), ('gla_gk', 16), ('gla_z', 512),
    ('ssd_x', 512), ('ssd_b', 256), ('ssd_c', 256), ('ssd_dt', 8), ('ssd_z', 512),
    ('nsa_q', 512), ('nsa_kc', 128), ('nsa_vc', 128), ('nsa_ks', 128), ('nsa_vs', 128),
    ('nsa_kw', 128), ('nsa_vw', 128), ('nsa_gate', 24), ('nsa_z', 512),
    ('merge_gate', 4096),
)
_SMALL_COL = {'gdn_beta': SM_BETA, 'gdn_a': SM_A, 'gla_gk': SM_GK, 'ssd_dt': SM_DT, 'nsa_gate': SM_GATE}


def _layout_weights(w_in):
    src, start = {}, 0
    for name, width in _IN_SPLITS:
        src[name] = (start, width)
        start += width
    take = lambda name: lax.slice_in_dim(w_in, src[name][0], src[name][0] + src[name][1], axis=w_in.ndim - 1)
    zeros = lambda n: jnp.zeros(w_in.shape[:-1] + (n,), w_in.dtype)
    pieces = []
    for name, (_, width) in _U_LAYOUT.items():
        if name == 'small':
            pos = 0
            for sname, col in sorted(_SMALL_COL.items(), key=lambda kv: kv[1]):
                if col > pos:
                    pieces.append(zeros(col - pos))
                pieces.append(take(sname))
                pos = col + src[sname][1]
            pieces.append(zeros(width - pos))
        elif name == 'pad':
            pieces.append(zeros(width))
        else:
            pieces.append(take(name))
    return jnp.concatenate(pieces, axis=-1)


def _mm(a, b):
    return jnp.dot(a.astype(_MXU), b.astype(_MXU), preferred_element_type=F32)


def _mm_nt(a, b):
    return lax.dot_general(a.astype(_MXU), b.astype(_MXU), (((1,), (1,)), ((), ())),
                           preferred_element_type=F32)


def _mm_tn(a, b):
    return lax.dot_general(a.astype(_MXU), b.astype(_MXU), (((0,), (0,)), ((), ())),
                           preferred_element_type=F32)


def _split_bf16(a):
    hi = a.astype(BF16)
    return hi, (a - hi.astype(F32)).astype(BF16)


def _mm3(a, b):
    ah, al = _split_bf16(a)
    bh, bl = _split_bf16(b)
    d = lambda x, y: jnp.dot(x, y, preferred_element_type=F32)
    return d(ah, bh) + d(ah, bl) + d(al, bh)


def _mm01(a01, b, b_contract=0):
    bh = b.astype(BF16)
    bm, bl = _split_bf16(b - bh.astype(F32))
    a = a01.astype(BF16)
    d = lambda y: lax.dot_general(a, y, (((1,), (b_contract,)), ((), ())), preferred_element_type=F32)
    return d(bh) + d(bm) + d(bl)


def _mm10(a, b01):
    ah = a.astype(BF16)
    am, al = _split_bf16(a - ah.astype(F32))
    b = b01.astype(BF16)
    d = lambda x: jnp.dot(x, b, preferred_element_type=F32)
    return d(ah) + d(am) + d(al)


def _silu(x):
    return x * jax.nn.sigmoid(x)


def _softplus(x):
    return jnp.maximum(x, 0.0) + jnp.log(1.0 + jnp.exp(-jnp.abs(x)))


def _iota2(shape, axis):
    return lax.broadcasted_iota(jnp.int32, shape, axis)


def _causal_conv_silu(xbuf, r0, col, width, w_ref, bias=None, rows=CHUNK):
    xx = xbuf[pl.ds(r0, rows + 8), col:col + width]
    w = w_ref[:, col:col + width]
    acc = xx * w[CONV_K - 1:CONV_K, :]
    for k in range(1, CONV_K):
        acc = acc + pltpu.roll(xx, k, 0) * w[CONV_K - 1 - k:CONV_K - k, :]
    y = acc[8:, :]
    if bias is not None:
        y = y + bias
    return _silu(y)


def _inproj_kernel(x_ref, g_ref, w_ref, o_ref, h_ref):
    @pl.when(pl.program_id(1) == 0)
    def _():
        x = x_ref[...]
        ms = jnp.mean(x * x, axis=-1, keepdims=True)
        h_ref[...] = (x * lax.rsqrt(ms + NORM_EPS) * g_ref[...]).astype(h_ref.dtype)

    o_ref[...] = jnp.dot(h_ref[...], w_ref[...], preferred_element_type=F32)


def _inproj(x2d, gain, w_all, *, tm=1024, tn=1408):
    t = x2d.shape[0]
    tm = min(tm, t)
    return pl.pallas_call(
        _inproj_kernel,
        grid=(t // tm, U_WIDTH // tn),
        in_specs=[
            pl.BlockSpec((tm, D_MODEL), lambda i, j: (i, 0)),
            pl.BlockSpec((1, D_MODEL), lambda i, j: (0, 0)),
            pl.BlockSpec((D_MODEL, tn), lambda i, j: (0, j)),
        ],
        out_specs=pl.BlockSpec((tm, tn), lambda i, j: (i, j)),
        out_shape=jax.ShapeDtypeStruct((t, U_WIDTH), F32),
        scratch_shapes=[pltpu.VMEM((tm, D_MODEL), _MXU)],
        compiler_params=pltpu.CompilerParams(
            dimension_semantics=("arbitrary", "arbitrary"), vmem_limit_bytes=_VMEM_LIMIT),
        name="inproj",
    )(x2d, gain.reshape(1, D_MODEL), w_all)


def _gdn_kernel(qkv_ref, z_ref, sm_ref, cw_ref, alog_ref, dtb_ref, onorm_ref, y_ref, xbuf, state, *, tc):
    c = CHUNK

    @pl.when(pl.program_id(1) == 0)
    def _():
        state[...] = jnp.zeros_like(state)
        xbuf[0:8, :] = jnp.zeros((8, xbuf.shape[1]), F32)

    xbuf[8:8 + tc, :] = qkv_ref[...]

    nch = tc // c
    ri, ci = _iota2((tc, tc), 0), _iota2((tc, tc), 1)
    same = (ri // c) == (ci // c)
    tril = same & (ri >= ci)
    stril = same & (ri > ci)
    tril_f = tril.astype(F32)
    eye_f = (ri == ci).astype(F32)
    neg_a = -jnp.exp(alog_ref[...])
    sm = sm_ref[...]
    beta_all = jax.nn.sigmoid(sm)
    g_all = neg_a * _softplus(sm + dtb_ref[...])
    onorm = onorm_ref[...]

    heads = range(GDN_HEADS)
    qs, ks, vbs, kbs, gcs, decays = [], [], [], [], [], []
    for h in heads:
        q = _causal_conv_silu(xbuf, 0, h * GDN_DK, GDN_DK, cw_ref, rows=tc)
        k = _causal_conv_silu(xbuf, 0, 512 + h * GDN_DK, GDN_DK, cw_ref, rows=tc)
        v = _causal_conv_silu(xbuf, 0, 1024 + h * GDN_DV, GDN_DV, cw_ref, rows=tc)
        q = q * lax.rsqrt(jnp.sum(q * q, axis=-1, keepdims=True) + NORM_EPS) * (GDN_DK ** -0.5)
        k = k * lax.rsqrt(jnp.sum(k * k, axis=-1, keepdims=True) + NORM_EPS)
        beta = jnp.broadcast_to(beta_all[:, SM_BETA + h:SM_BETA + h + 1], (tc, GDN_DK))
        gb = jnp.broadcast_to(g_all[:, SM_A + h:SM_A + h + 1], (tc, GDN_DK))
        gc = _mm01(tril_f, gb)
        gcw = jnp.concatenate([gc] * (tc // GDN_DK), axis=1)
        qs.append(q)
        ks.append(k)
        vbs.append(v * beta)
        kbs.append(k * beta)
        gcs.append(gc)
        decays.append(jnp.exp(jnp.where(tril, gcw - gcw.T, -jnp.inf)))
    lowers = [jnp.where(stril, _mm_nt(kbs[h], ks[h]) * decays[h], 0.0) for h in heads]
    npows = [-lo for lo in lowers]
    invs = [eye_f + n for n in npows]
    for _ in range(5):
        npows = [_mm(n, n) for n in npows]
        invs = [x + _mm(x, n) for x, n in zip(invs, npows)]
    egcs = [jnp.exp(gc) for gc in gcs]
    rhss = [jnp.concatenate([vbs[h], kbs[h] * egcs[h]], axis=1) for h in heads]
    sols = [_mm(invs[h], rhss[h]) for h in heads]
    resids = [rhss[h] - (sols[h] + _mm3(lowers[h], sols[h])) for h in heads]
    sols = [sols[h] + _mm(invs[h], resids[h]) for h in heads]
    attns = [_mm_nt(qs[h], ks[h]) * decays[h] for h in heads]
    qgs = [qs[h] * egcs[h] for h in heads]
    gc_lasts = [jnp.broadcast_to(gc.reshape(nch, c, GDN_DK)[:, c - 1:c, :], (nch, c, GDN_DK)).reshape(tc, GDN_DK)
                for gc in gcs]
    kds = [ks[h] * jnp.exp(gc_lasts[h] - gcs[h]) for h in heads]
    decs = [jnp.exp(g) for g in gc_lasts]
    sts = [state[h] for h in heads]
    v_news = [[] for _ in heads]
    o_inter = [[] for _ in heads]
    for ck in range(nch):
        rows = slice(ck * c, (ck + 1) * c)
        for h in heads:
            v_new = sols[h][rows, :GDN_DV] - _mm(sols[h][rows, GDN_DV:], sts[h])
            o_inter[h].append(_mm(qgs[h][rows], sts[h]))
            sts[h] = sts[h] * decs[h][ck * c:ck * c + 1, :] + _mm_tn(kds[h][rows], v_new)
            v_news[h].append(v_new)
    for h in heads:
        state[h] = sts[h]
        o = jnp.concatenate(o_inter[h], axis=0) + _mm(attns[h], jnp.concatenate(v_news[h], axis=0))
        o = o * lax.rsqrt(jnp.mean(o * o, axis=-1, keepdims=True) + NORM_EPS) * onorm
        y_ref[:, h * GDN_DV:(h + 1) * GDN_DV] = o * _silu(z_ref[:, h * GDN_DV:(h + 1) * GDN_DV])

    xbuf[0:8, :] = qkv_ref[tc - 8:tc, :]


def _small_row(vec, col):
    return jnp.zeros((1, 128), F32).at[0, col:col + vec.shape[0]].set(vec.astype(F32))


def _gdn(u_all, bsz, s_len, conv_w, a_log, dt_bias, onorm, *, tc=256):
    tc = min(tc, s_len)
    ns = s_len // tc
    sm_blk = _U_LAYOUT['small'][0] // 128
    return pl.pallas_call(
        functools.partial(_gdn_kernel, tc=tc),
        grid=(bsz, ns),
        in_specs=[
            pl.BlockSpec((tc, 1536), lambda b, s: (b * ns + s, 0)),
            pl.BlockSpec((tc, 512), lambda b, s: (b * ns + s, _U_LAYOUT['gdn_z'][0] // 512)),
            pl.BlockSpec((tc, 128), lambda b, s: (b * ns + s, sm_blk)),
            pl.BlockSpec((CONV_K, 1536), lambda b, s: (0, 0)),
            pl.BlockSpec((1, 128), lambda b, s: (0, 0)),
            pl.BlockSpec((1, 128), lambda b, s: (0, 0)),
            pl.BlockSpec((1, GDN_DV), lambda b, s: (0, 0)),
        ],
        out_specs=pl.BlockSpec((tc, 512), lambda b, s: (b * ns + s, 0)),
        out_shape=jax.ShapeDtypeStruct((bsz * s_len, 512), F32),
        scratch_shapes=[pltpu.VMEM((tc + 8, 1536), F32), pltpu.VMEM((GDN_HEADS, GDN_DK, GDN_DV), F32)],
        compiler_params=pltpu.CompilerParams(
            dimension_semantics=("arbitrary", "arbitrary"), vmem_limit_bytes=_VMEM_LIMIT),
        name="gdn",
    )(u_all, u_all, u_all, conv_w.astype(F32), _small_row(a_log, SM_A), _small_row(dt_bias, SM_A),
      onorm.reshape(1, GDN_DV).astype(F32))


def _gla_kernel(qk_ref, v_ref, z_ref, sm_ref, wgk_ref, bgk_ref, onorm_ref, y_ref, state_t, *, tc):
    c = CHUNK

    @pl.when(pl.program_id(1) == 0)
    def _():
        state_t[...] = jnp.zeros_like(state_t)

    nch = tc // c
    nk = GLA_HEADS * GLA_DK
    ri, ci = _iota2((tc, tc), 0), _iota2((tc, tc), 1)
    tril = ((ri // c) == (ci // c)) & (ri >= ci)
    onorm = onorm_ref[...]
    heads = range(GLA_HEADS)

    pre = _mm(sm_ref[...], wgk_ref[...]) + bgk_ref[...]
    gk = (jnp.minimum(pre, 0.0) - jnp.log(1.0 + jnp.exp(-jnp.abs(pre)))) * (1.0 / GLA_GATE_NORMALIZER)
    b = _mm01(tril.astype(F32), gk)
    b3 = b.reshape(nch, c, nk)
    at_row = lambda r: jnp.broadcast_to(b3[:, r:r + 1, :], (nch, c, nk)).reshape(tc, nk)
    bref, b_last = at_row(c // 2), at_row(c - 1)
    q = qk_ref[:, 0:nk] * (GLA_DK ** -0.5)
    k = qk_ref[:, nk:2 * nk]
    q_in = q * jnp.exp(b - bref)
    k_in = k * jnp.exp(bref - b)
    qg = q * jnp.exp(b)
    kd = k * jnp.exp(b_last - b)
    dec = jnp.exp(b_last)
    hs = lambda x, h: x[:, h * GLA_DK:(h + 1) * GLA_DK]
    vs = [v_ref[:, h * GLA_DV:(h + 1) * GLA_DV] for h in heads]
    a_intra = [jnp.where(tril, _mm_nt(hs(q_in, h), hs(k_in, h)), 0.0) for h in heads]
    o_intra = [_mm(a_intra[h], vs[h]) for h in heads]
    contrib = [[_mm_tn(vs[h][ck * c:(ck + 1) * c], hs(kd, h)[ck * c:(ck + 1) * c]) for ck in range(nch)]
               for h in heads]
    for h in heads:
        st = state_t[h]
        o_inter = []
        for ck in range(nch):
            o_inter.append(_mm_nt(hs(qg, h)[ck * c:(ck + 1) * c], st))
            st = st * hs(dec, h)[ck * c:ck * c + 1, :] + contrib[h][ck]
        state_t[h] = st
        o = o_intra[h] + jnp.concatenate(o_inter, axis=0)
        o = o * lax.rsqrt(jnp.mean(o * o, axis=-1, keepdims=True) + NORM_EPS) * onorm
        y_ref[:, h * GLA_DV:(h + 1) * GLA_DV] = o * _silu(z_ref[:, h * GLA_DV:(h + 1) * GLA_DV])


def _gla(u_all, bsz, s_len, w_gk, b_gk, onorm, *, tc=256):
    tc = min(tc, s_len)
    ns = s_len // tc
    sm_blk = _U_LAYOUT['small'][0] // 128
    w_pad = jnp.zeros((128, GLA_HEADS * GLA_DK), F32).at[SM_GK:SM_GK + GLA_LOWRANK].set(w_gk.astype(F32))
    row = lambda b, s: b * ns + s
    return pl.pallas_call(
        functools.partial(_gla_kernel, tc=tc),
        grid=(bsz, ns),
        in_specs=[
            pl.BlockSpec((tc, 512), lambda b, s: (row(b, s), _U_LAYOUT['gla_q'][0] // 512)),
            pl.BlockSpec((tc, 512), lambda b, s: (row(b, s), _U_LAYOUT['gla_v'][0] // 512)),
            pl.BlockSpec((tc, 512), lambda b, s: (row(b, s), _U_LAYOUT['gla_z'][0] // 512)),
            pl.BlockSpec((tc, 128), lambda b, s: (row(b, s), sm_blk)),
            pl.BlockSpec((128, 256), lambda b, s: (0, 0)),
            pl.BlockSpec((1, 256), lambda b, s: (0, 0)),
            pl.BlockSpec((1, GLA_DV), lambda b, s: (0, 0)),
        ],
        out_specs=pl.BlockSpec((tc, 512), lambda b, s: (row(b, s), 0)),
        out_shape=jax.ShapeDtypeStruct((bsz * s_len, 512), F32),
        scratch_shapes=[pltpu.VMEM((GLA_HEADS, GLA_DV, GLA_DK), F32)],
        compiler_params=pltpu.CompilerParams(
            dimension_semantics=("arbitrary", "arbitrary"), vmem_limit_bytes=_VMEM_LIMIT),
        name="gla",
    )(u_all, u_all, u_all, u_all, w_pad, b_gk.reshape(1, -1).astype(F32), onorm.reshape(1, GLA_DV).astype(F32))


def _ssd_kernel(x_ref, bc_ref, z_ref, sm_ref, cw_ref, cb_ref, alog_ref, dtb_ref, dskip_ref, onorm_ref,
                y_ref, xbuf, state, *, tc):
    c = CHUNK
    hg = SSD_HEADS // SSD_GROUPS

    @pl.when(pl.program_id(1) == 0)
    def _():
        state[...] = jnp.zeros_like(state)
        xbuf[0:8, :] = jnp.zeros((8, xbuf.shape[1]), F32)

    xbuf[8:8 + tc, 0:512] = x_ref[...]
    xbuf[8:8 + tc, 512:1024] = bc_ref[...]

    nch = tc // c
    inner = SSD_HEADS * SSD_HEADDIM
    ri, ci = _iota2((tc, tc), 0), _iota2((tc, tc), 1)
    tril = ((ri // c) == (ci // c)) & (ri >= ci)
    neg_a = -jnp.exp(alog_ref[...])
    cbias = cb_ref[...]
    heads = range(SSD_HEADS)

    dt_all = _softplus(sm_ref[...] + dtb_ref[...])
    acs_all = _mm01(tril.astype(F32), dt_all * neg_a)
    acs_last_all = jnp.broadcast_to(acs_all.reshape(nch, c, 128)[:, c - 1:c, :], (nch, c, 128)).reshape(tc, 128)
    spread_m = (_iota2((128, inner), 0) == SM_DT + _iota2((128, inner), 1) // SSD_HEADDIM).astype(F32)
    spread = lambda cols: _mm10(cols, spread_m)
    xs = _causal_conv_silu(xbuf, 0, 0, inner, cw_ref, cbias[:, 0:inner], rows=tc)
    xdt = xs * spread(dt_all)
    xdec = xdt * spread(jnp.exp(acs_last_all - acs_all))
    bms, cms, cbs = [], [], []
    for g in range(SSD_GROUPS):
        lo = inner + g * SSD_STATE
        bms.append(_causal_conv_silu(xbuf, 0, lo, SSD_STATE, cw_ref, cbias[:, lo:lo + SSD_STATE], rows=tc))
        lo = inner + SSD_GROUPS * SSD_STATE + g * SSD_STATE
        cms.append(_causal_conv_silu(xbuf, 0, lo, SSD_STATE, cw_ref, cbias[:, lo:lo + SSD_STATE], rows=tc))
        cbs.append(_mm_nt(cms[g], bms[g]))
    hl = lambda x, hh: x[:, hh * SSD_HEADDIM:(hh + 1) * SSD_HEADDIM]
    lmats = []
    for hh in heads:
        acs_b = jnp.broadcast_to(acs_all[:, SM_DT + hh:SM_DT + hh + 1], (tc, tc))
        lmats.append(jnp.exp(jnp.where(tril, acs_b - acs_b.T, -jnp.inf)))
    y_diag = [_mm(cbs[hh // hg] * lmats[hh], hl(xdt, hh)) for hh in heads]
    contrib = [[_mm_tn(hl(xdec, hh)[ck * c:(ck + 1) * c], bms[hh // hg][ck * c:(ck + 1) * c]) for ck in range(nch)]
               for hh in heads]
    cdec = jnp.exp(acs_last_all)
    y_off = []
    for hh in heads:
        st = state[hh]
        parts = []
        for ck in range(nch):
            parts.append(_mm_nt(cms[hh // hg][ck * c:(ck + 1) * c], st))
            st = st * cdec[ck * c:ck * c + 1, SM_DT + hh:SM_DT + hh + 1] + contrib[hh][ck]
        state[hh] = st
        y_off.append(jnp.concatenate(parts, axis=0))
    y = (jnp.concatenate(y_diag, axis=1) + jnp.concatenate(y_off, axis=1) * spread(jnp.exp(acs_all))
         + xs * dskip_ref[...])
    y = y * _silu(z_ref[...])
    y_ref[...] = y * lax.rsqrt(jnp.mean(y * y, axis=-1, keepdims=True) + NORM_EPS) * onorm_ref[...]
    xbuf[0:8, 0:512] = x_ref[tc - 8:tc, :]
    xbuf[0:8, 512:1024] = bc_ref[tc - 8:tc, :]


def _ssd(u_all, bsz, s_len, conv_w, conv_b, a_log, dt_bias, d_skip, onorm, *, tc=256):
    tc = min(tc, s_len)
    ns = s_len // tc
    sm_blk = _U_LAYOUT['small'][0] // 128
    row = lambda b, s: b * ns + s
    const = lambda b, s: (0, 0)
    return pl.pallas_call(
        functools.partial(_ssd_kernel, tc=tc),
        grid=(bsz, ns),
        in_specs=[
            pl.BlockSpec((tc, 512), lambda b, s: (row(b, s), _U_LAYOUT['ssd_x'][0] // 512)),
            pl.BlockSpec((tc, 512), lambda b, s: (row(b, s), _U_LAYOUT['ssd_b'][0] // 512)),
            pl.BlockSpec((tc, 512), lambda b, s: (row(b, s), _U_LAYOUT['ssd_z'][0] // 512)),
            pl.BlockSpec((tc, 128), lambda b, s: (row(b, s), sm_blk)),
            pl.BlockSpec((CONV_K, 1024), const),
            pl.BlockSpec((1, 1024), const),
            pl.BlockSpec((1, 128), const),
            pl.BlockSpec((1, 128), const),
            pl.BlockSpec((1, 512), const),
            pl.BlockSpec((1, 512), const),
        ],
        out_specs=pl.BlockSpec((tc, 512), lambda b, s: (row(b, s), 0)),
        out_shape=jax.ShapeDtypeStruct((bsz * s_len, 512), F32),
        scratch_shapes=[pltpu.VMEM((tc + 8, 1024), F32), pltpu.VMEM((SSD_HEADS, SSD_HEADDIM, SSD_STATE), F32)],
        compiler_params=pltpu.CompilerParams(
            dimension_semantics=("arbitrary", "arbitrary"), vmem_limit_bytes=_VMEM_LIMIT),
        name="ssd",
    )(u_all, u_all, u_all, u_all, conv_w.astype(F32), conv_b.reshape(1, -1).astype(F32),
      _small_row(a_log, SM_DT), _small_row(dt_bias, SM_DT),
      jnp.repeat(d_skip.astype(F32), SSD_HEADDIM).reshape(1, 512), onorm.reshape(1, 512).astype(F32))


def _compress_kernel(kc_ref, vc_ref, pos_ref, w1_ref, w2_ref, o_ref, *, nsub):
    for kv, ref in enumerate((kc_ref, vc_ref)):
        e = jnp.zeros((nsub, 128), F32)
        f = jnp.zeros((nsub, 128), F32)
        for r in range(CMP_STRIDE):
            x = ref[pl.ds(r, nsub, stride=CMP_STRIDE), :]
            e = e + _mm(x + pos_ref[kv, r], w1_ref[kv, r])
            f = f + _mm(x + pos_ref[kv, CMP_STRIDE + r], w1_ref[kv, CMP_STRIDE + r])
        pre = e + pltpu.roll(f, nsub - 1, 0)
        o_ref[kv, 0] = _mm(_silu(pre), w2_ref[kv])


def _per_group(w):
    eye = jnp.eye(NSA_KV_HEADS, dtype=w.dtype)
    out = jnp.einsum('gh,...ij->...gihj', eye, w)
    return out.reshape(w.shape[:-2] + (NSA_KV_HEADS * w.shape[-2], NSA_KV_HEADS * w.shape[-1]))


def _compress(u_all, bsz, s_len, pos, w1, w2):
    hd = NSA_HEADDIM
    nsub = s_len // CMP_STRIDE
    w1_bd = _per_group(w1.reshape(2, CMP_BLOCK, hd, hd)).astype(_MXU)
    pos_bd = jnp.tile(pos.astype(F32), (1, 1, NSA_KV_HEADS)).reshape(2, CMP_BLOCK, 1, NSA_KV_HEADS * hd)
    const = lambda n: (lambda b: (0,) * n)
    return pl.pallas_call(
        functools.partial(_compress_kernel, nsub=nsub),
        grid=(bsz,),
        in_specs=[
            pl.BlockSpec((s_len, 128), lambda b: (b, _U_LAYOUT['nsa_kc'][0] // 128)),
            pl.BlockSpec((s_len, 128), lambda b: (b, _U_LAYOUT['nsa_vc'][0] // 128)),
            pl.BlockSpec(pos_bd.shape, const(4)),
            pl.BlockSpec(w1_bd.shape, const(4)),
            pl.BlockSpec((2, 128, 128), const(3)),
        ],
        out_specs=pl.BlockSpec((2, 1, nsub, 128), lambda b: (0, b, 0, 0)),
        out_shape=jax.ShapeDtypeStruct((2, bsz, nsub, 128), F32),
        compiler_params=pltpu.CompilerParams(dimension_semantics=("arbitrary",), vmem_limit_bytes=_VMEM_LIMIT),
        name="nsa_compress",
    )(u_all, u_all, pos_bd, w1_bd, _per_group(w2).astype(_MXU))


def _kvprep_kernel(ks_ref, vs_ref, kw_ref, vw_ref, fs_ref, fw_ref, ksel_ref, vsel_ref, kwin_ref, vwin_ref):
    hd = NSA_HEADDIM
    low = _iota2(ks_ref.shape, 1) < hd
    for k_ref, f_ref, o_ref in ((ks_ref, fs_ref, ksel_ref), (kw_ref, fw_ref, kwin_ref)):
        k = k_ref[...]
        f = f_ref[...]
        o_ref[0] = jnp.where(low, k, f).astype(o_ref.dtype)
        o_ref[1] = jnp.where(low, pltpu.roll(k, hd, 1), f).astype(o_ref.dtype)
    for v_ref, o_ref in ((vs_ref, vsel_ref), (vw_ref, vwin_ref)):
        vt = v_ref[...].T
        o_ref[0, 0] = vt[:hd].astype(o_ref.dtype)
        o_ref[1, 0] = vt[hd:].astype(o_ref.dtype)


def _kvprep(u_all, bsz, s_len, feat_sel, feat_win, *, tk):
    hd, ngrp = NSA_HEADDIM, NSA_KV_HEADS
    nt = s_len // tk
    ucol = lambda name: pl.BlockSpec((tk, 128), lambda b, j: (b * nt + j, _U_LAYOUT[name][0] // 128))
    fspec = pl.BlockSpec((tk, 128), lambda b, j: (j, 0))
    kspec = pl.BlockSpec((ngrp, tk, 128), lambda b, j: (b, j, 0))
    vspec = pl.BlockSpec((ngrp, 1, hd, tk), lambda b, j: (b, j, 0, 0))
    kshape = jax.ShapeDtypeStruct((bsz * ngrp, s_len, 128), _MXU)
    vshape = jax.ShapeDtypeStruct((bsz * ngrp, nt, hd, tk), _MXU)
    return pl.pallas_call(
        _kvprep_kernel,
        grid=(bsz, nt),
        in_specs=[ucol('nsa_ks'), ucol('nsa_vs'), ucol('nsa_kw'), ucol('nsa_vw'), fspec, fspec],
        out_specs=[kspec, vspec, kspec, vspec],
        out_shape=[kshape, vshape, kshape, vshape],
        compiler_params=pltpu.CompilerParams(
            dimension_semantics=("arbitrary", "arbitrary"), vmem_limit_bytes=_VMEM_LIMIT),
        name="nsa_kv_prep",
    )(u_all, u_all, u_all, u_all, feat_sel, feat_win)


def _cmpattn_kernel(slopes_ref, q_ref, kc_ref, vc_ref, ovt_ref, oc_ref, qsel_ref, *, tq, nsel, nc, topk):
    hd, ngrp = NSA_HEADDIM, NSA_KV_HEADS
    hg = NSA_HEADS // ngrp
    q0 = pl.program_id(1) * tq
    nsub = kc_ref.shape[2]
    t = q0 + _iota2((tq, nsub), 0)
    ccol = _iota2((tq, nsub), 1)
    cpos = ccol * CMP_STRIDE + (CMP_BLOCK - 1)
    valid = (cpos <= t) & (ccol < nc)
    dist = (t - cpos).astype(F32)
    lane = _iota2((tq, 2 * hd), 1)
    kcs = [kc_ref[0, 0][:, g * hd:(g + 1) * hd] for g in range(ngrp)]
    vcs = [vc_ref[0, 0][:, g * hd:(g + 1) * hd] for g in range(ngrp)]
    heads = [(g, h) for g in range(ngrp) for h in range(hg)]

    def q_slab(idx):
        slab = q_ref[:, (idx // 2) * 2 * hd:(idx // 2 + 1) * 2 * hd] * (hd ** -0.5)
        return slab if idx % 2 == 0 else pltpu.roll(slab, hd, 1)

    slabs = [q_slab(g * hg + h) for g, h in heads]
    ss = [_mm_nt(slabs[i][:, :hd], kcs[g]) - slopes_ref[g * hg + h] * dist for i, (g, h) in enumerate(heads)]
    ss = [jnp.where(valid, s, -jnp.inf) for s in ss]
    ms = [jnp.max(s, axis=-1, keepdims=True) for s in ss]
    es = [jnp.exp(s - jnp.where(m > -jnp.inf, m, 0.0)) for s, m in zip(ss, ms)]
    ps = [e / jnp.maximum(jnp.sum(e, axis=-1, keepdims=True), 1e-30) for e in es]
    for i, (g, h) in enumerate(heads):
        oc_ref[g, :, h * hd:(h + 1) * hd] = _mm(ps[i], vcs[g])
    jj = _iota2((nsel, tq), 0)
    cur = (q0 + _iota2((nsel, tq), 1)) // SEL_BLOCK
    forced = (jj == 0) | (jj == cur) | (jj == cur - 1)
    for g in range(ngrp):
        psum = ps[g * hg]
        for h in range(1, hg):
            psum = psum + ps[g * hg + h]
        imp = _mm01(ovt_ref[...], psum, b_contract=1)
        imp = jnp.where(jj <= cur, jnp.where(forced, FORCED_SCORE, imp), -1.0)
        rank = jnp.zeros((nsel, tq), jnp.int32)
        for j2 in range(nsel):
            row = imp[j2:j2 + 1, :]
            before = (row > imp) | ((row == imp) & (jj > j2))
            rank = rank + before.astype(jnp.int32)
        picked = (rank < topk) & (imp >= 0.0)
        bias_t = jnp.where(picked, 0.0, _UNSELECTED)
        pieces = [jnp.zeros((hd, tq), F32), bias_t]
        if nsel < hd:
            pieces.append(jnp.zeros((hd - nsel, tq), F32))
        bias = jnp.concatenate(pieces, axis=0).T
        for h in range(hg):
            feat = bias + jnp.where(lane == hd, slopes_ref[g * hg + h], 0.0)
            qsel_ref[g, h] = jnp.where(lane < hd, slabs[g * hg + h], feat).astype(qsel_ref.dtype)


def _cmpattn(slopes, u_all, cmp, ovt, bsz, s_len, *, nc, tq=256):
    hd, ngrp = NSA_HEADDIM, NSA_KV_HEADS
    hg = NSA_HEADS // ngrp
    tq = min(tq, s_len)
    nq = s_len // tq
    nsel = s_len // SEL_BLOCK
    nsub = cmp.shape[2]
    qw = NSA_HEADS * hd
    return pl.pallas_call(
        functools.partial(_cmpattn_kernel, tq=tq, nsel=nsel, nc=nc, topk=min(SEL_TOPK, nsel)),
        grid=(bsz, nq),
        in_specs=[
            pl.BlockSpec(memory_space=pltpu.SMEM),
            pl.BlockSpec((tq, qw), lambda i, j: (i * nq + j, _U_LAYOUT['nsa_q'][0] // qw)),
            pl.BlockSpec((1, 1, nsub, ngrp * hd), lambda i, j: (0, i, 0, 0)),
            pl.BlockSpec((1, 1, nsub, ngrp * hd), lambda i, j: (1, i, 0, 0)),
            pl.BlockSpec((nsel, nsub), lambda i, j: (0, 0)),
        ],
        out_specs=[
            pl.BlockSpec((ngrp, tq, hg * hd), lambda i, j: (i, j, 0)),
            pl.BlockSpec((ngrp, hg, tq, 2 * hd), lambda i, j: (i, 0, j, 0)),
        ],
        out_shape=[
            jax.ShapeDtypeStruct((bsz * ngrp, s_len, hg * hd), F32),
            jax.ShapeDtypeStruct((bsz * ngrp, hg, s_len, 2 * hd), _MXU),
        ],
        compiler_params=pltpu.CompilerParams(
            dimension_semantics=("arbitrary", "arbitrary"), vmem_limit_bytes=_VMEM_LIMIT),
        name="nsa_cmp_attn",
    )(slopes, u_all, cmp, cmp, ovt)


def _flash_tiles(q_alls, k_ref, vt_ref, slope_rows, q0, spans, *, tq, tk, hg, window):
    rel = _iota2((tk, tq), 1) - _iota2((tk, tq), 0)
    ngrp = len(q_alls)

    def step(masked, kt, carries):
        k0 = pl.multiple_of(kt * tk, tk)
        if masked:
            d = rel + (q0 - k0)
            ok = d >= 0
            if window is not None:
                ok = ok & (d < window)
            mb = jnp.where(ok, 0.0, _MASKED)
            mb = jnp.concatenate([mb] * hg, axis=1)
        grps = range(ngrp)
        sts = [_mm_nt(k_ref[g, pl.ds(k0, tk), :], q_alls[g]) for g in grps]
        if masked:
            sts = [st + mb for st in sts]
        offs = [slope_rows[g] * k0.astype(F32) for g in grps]
        m_news = [jnp.maximum(carries[g][0], jnp.max(sts[g], axis=0, keepdims=True) + offs[g]) for g in grps]
        ps = [jnp.exp(sts[g] - (m_news[g] - offs[g])) for g in grps]
        alphas = [jnp.exp(carries[g][0] - m_news[g]) for g in grps]
        ls = [alphas[g] * carries[g][1] + jnp.sum(ps[g], axis=0, keepdims=True) for g in grps]
        accs = [alphas[g] * carries[g][2] + _mm(vt_ref[g, kt], ps[g]) for g in grps]
        return tuple((m_news[g], ls[g], accs[g]) for g in grps)

    rows = hg * tq
    init = (jnp.full((1, rows), _MASKED, F32), jnp.zeros((1, rows), F32), jnp.zeros((NSA_HEADDIM, rows), F32))
    carries = (init,) * ngrp
    lo, a, b, hi = spans
    carries = lax.fori_loop(lo, a, functools.partial(step, True), carries)
    carries = lax.fori_loop(a, b, functools.partial(step, False), carries)
    carries = lax.fori_loop(b, hi, functools.partial(step, True), carries)
    return [acc / l for _, l, acc in carries]


def _slope_row(slopes_ref, g, hg, tq):
    return jnp.concatenate([jnp.full((1, tq), slopes_ref[g * hg + h], F32) for h in range(hg)], axis=1)


def _heads_to_lanes(ot, hg, tq):
    pairs = []
    for h in range(0, hg, 2):
        two = jnp.concatenate([ot[:, h * tq:(h + 1) * tq], ot[:, (h + 1) * tq:(h + 2) * tq]], axis=0)
        pairs.append(two.T)
    return jnp.concatenate(pairs, axis=1)


def _selattn_kernel(slopes_ref, q_ref, k_ref, vt_ref, o_ref, *, tq, tk):
    ngrp, hg = NSA_KV_HEADS, NSA_HEADS // NSA_KV_HEADS
    q0 = pl.program_id(1) * tq
    full_hi = (q0 + 1) // tk
    hi = (q0 + tq - 1) // tk + 1
    q_alls = [q_ref[g].reshape(hg * tq, q_ref.shape[-1]) for g in range(ngrp)]
    slope_rows = [_slope_row(slopes_ref, g, hg, tq) for g in range(ngrp)]
    ots = _flash_tiles(q_alls, k_ref, vt_ref, slope_rows, q0, (0, 0, full_hi, hi),
                       tq=tq, tk=tk, hg=hg, window=None)
    for g in range(ngrp):
        o_ref[g] = _heads_to_lanes(ots[g], hg, tq)


def _selattn(slopes, qsel, ksel, vsel_t, *, tq, tk):
    bg, hg, s_len, wid = qsel.shape
    hd, ngrp = NSA_HEADDIM, NSA_KV_HEADS
    return pl.pallas_call(
        functools.partial(_selattn_kernel, tq=tq, tk=tk),
        grid=(bg // ngrp, s_len // tq),
        in_specs=[
            pl.BlockSpec(memory_space=pltpu.SMEM),
            pl.BlockSpec((ngrp, hg, tq, wid), lambda i, j: (i, 0, j, 0)),
            pl.BlockSpec((ngrp, s_len, wid), lambda i, j: (i, 0, 0)),
            pl.BlockSpec((ngrp, s_len // tk, hd, tk), lambda i, j: (i, 0, 0, 0)),
        ],
        out_specs=pl.BlockSpec((ngrp, tq, hg * hd), lambda i, j: (i, j, 0)),
        out_shape=jax.ShapeDtypeStruct((bg, s_len, hg * hd), F32),
        compiler_params=pltpu.CompilerParams(
            dimension_semantics=("arbitrary", "arbitrary"), vmem_limit_bytes=_VMEM_LIMIT),
        name="nsa_sel_attn",
    )(slopes, qsel, ksel, vsel_t)


def _winattn_kernel(slopes_ref, q_ref, k_ref, vt_ref, oc_ref, os_ref, gate_ref, gexp_ref, z0_ref, z1_ref, y_ref,
                    *, tq, tk):
    ngrp, hg = NSA_KV_HEADS, NSA_HEADS // NSA_KV_HEADS
    gw = hg * NSA_HEADDIM
    q0 = pl.program_id(1) * tq
    lo = jnp.maximum(q0 - (WINDOW - 1), 0) // tk
    full_hi = (q0 + 1) // tk
    full_lo = jnp.minimum(jnp.maximum(q0 + tq - WINDOW + tk - 1, 0) // tk, full_hi)
    hi = (q0 + tq - 1) // tk + 1
    q_alls = [q_ref[g].reshape(hg * tq, q_ref.shape[-1]) for g in range(ngrp)]
    slope_rows = [_slope_row(slopes_ref, g, hg, tq) for g in range(ngrp)]
    ots = _flash_tiles(q_alls, k_ref, vt_ref, slope_rows, q0, (lo, full_lo, full_hi, hi),
                       tq=tq, tk=tk, hg=hg, window=WINDOW)
    gates = jax.nn.sigmoid(gate_ref[...])
    for g, z_ref in enumerate((z0_ref, z1_ref)):
        o_w = _heads_to_lanes(ots[g], hg, tq)
        spread = lambda c: _mm10(gates, gexp_ref[g, c])
        o = spread(0) * oc_ref[g] + spread(1) * os_ref[g] + spread(2) * o_w
        y_ref[:, g * gw:(g + 1) * gw] = o * _silu(z_ref[...])


def _winattn(slopes, q_pad, kwin, vwin_t, o_c, o_s, u_all, *, tq, tk):
    bg, hg, s_len, wid = q_pad.shape
    hd, ngrp = NSA_HEADDIM, NSA_KV_HEADS
    nq = s_len // tq
    gw = hg * hd
    zblk = _U_LAYOUT['nsa_z'][0] // gw
    head = jnp.arange(ngrp)[:, None, None, None] * hg + jnp.arange(gw)[None, None, None, :] // hd
    gexp = (jnp.arange(128)[None, None, :, None] == SM_GATE + 3 * head + jnp.arange(3)[None, :, None, None]).astype(F32)
    return pl.pallas_call(
        functools.partial(_winattn_kernel, tq=tq, tk=tk),
        grid=(bg // ngrp, nq),
        in_specs=[
            pl.BlockSpec(memory_space=pltpu.SMEM),
            pl.BlockSpec((ngrp, hg, tq, wid), lambda i, j: (i, 0, j, 0)),
            pl.BlockSpec((ngrp, s_len, wid), lambda i, j: (i, 0, 0)),
            pl.BlockSpec((ngrp, s_len // tk, hd, tk), lambda i, j: (i, 0, 0, 0)),
            pl.BlockSpec((ngrp, tq, gw), lambda i, j: (i, j, 0)),
            pl.BlockSpec((ngrp, tq, gw), lambda i, j: (i, j, 0)),
            pl.BlockSpec((tq, 128), lambda i, j: (i * nq + j, _U_LAYOUT['small'][0] // 128)),
            pl.BlockSpec((ngrp, 3, 128, gw), lambda i, j: (0, 0, 0, 0)),
            pl.BlockSpec((tq, gw), lambda i, j: (i * nq + j, zblk)),
            pl.BlockSpec((tq, gw), lambda i, j: (i * nq + j, zblk + 1)),
        ],
        out_specs=pl.BlockSpec((tq, ngrp * gw), lambda i, j: (i * nq + j, 0)),
        out_shape=jax.ShapeDtypeStruct((bg // ngrp * s_len, ngrp * gw), F32),
        compiler_params=pltpu.CompilerParams(
            dimension_semantics=("arbitrary", "arbitrary"), vmem_limit_bytes=_VMEM_LIMIT),
        name="nsa_win_attn",
    )(slopes, q_pad, kwin, vwin_t, o_c, o_s, u_all, gexp, u_all, u_all)


def _nsa(u_all, bsz, s_len, cmp_pos_k, cmp_pos_v, w_ck1, w_ck2, w_cv1, w_cv2):
    hd = NSA_HEADDIM
    tq, tk = min(_NSA_TQ, s_len), min(_NSA_TK, s_len)
    slopes = 2.0 ** (-8.0 * jnp.arange(1, NSA_HEADS + 1, dtype=F32) / NSA_HEADS)
    nsel = s_len // SEL_BLOCK
    pos = jnp.arange(s_len)
    tile_off = (pos % tk).astype(F32)[:, None]
    onehot = (pos[:, None] // SEL_BLOCK == jnp.arange(1, hd)[None, :]).astype(F32)
    feat_sel = jnp.concatenate([jnp.zeros((s_len, hd), F32), tile_off, onehot], axis=-1)
    feat_win = jnp.concatenate([jnp.zeros((s_len, hd), F32), tile_off, jnp.zeros((s_len, hd - 1), F32)], axis=-1)
    ksel, vsel_t, kwin, vwin_t = _kvprep(u_all, bsz, s_len, feat_sel, feat_win, tk=tk)
    nsub = s_len // CMP_STRIDE
    nc = nsub - CMP_BLOCK // CMP_STRIDE + 1
    cmp = _compress(u_all, bsz, s_len, jnp.stack([cmp_pos_k, cmp_pos_v]), jnp.stack([w_ck1, w_cv1]),
                    jnp.stack([w_ck2, w_cv2]))
    cmp_start = jnp.arange(nsub) * CMP_STRIDE
    sel_start = jnp.arange(nsel) * SEL_BLOCK
    ovt = ((cmp_start[None, :] <= sel_start[:, None] + SEL_BLOCK - 1)
           & (cmp_start[None, :] + CMP_BLOCK - 1 >= sel_start[:, None])
           & (jnp.arange(nsub)[None, :] < nc)).astype(F32)
    o_c, qsel = _cmpattn(slopes, u_all, cmp, ovt, bsz, s_len, nc=nc, tq=tq)
    o_s = _selattn(slopes, qsel, ksel, vsel_t, tq=tq, tk=tk)
    return _winattn(slopes, qsel, kwin, vwin_t, o_c, o_s, u_all, tq=tq, tk=tk)


def _merge_kernel(y0_ref, y1_ref, y2_ref, y3_ref, g0_ref, g1_ref, g2_ref, g3_ref, wbr_ref, wout_ref, gain_ref,
                  x_ref, o_ref):
    merged = None
    for n, (y_ref, g_ref) in enumerate(((y0_ref, g0_ref), (y1_ref, g1_ref), (y2_ref, g2_ref), (y3_ref, g3_ref))):
        term = jax.nn.sigmoid(g_ref[...]) * jnp.dot(y_ref[...].astype(_MXU), wbr_ref[n], preferred_element_type=F32)
        merged = term if merged is None else merged + term
    out = jnp.dot(merged.astype(_MXU), wout_ref[...], preferred_element_type=F32)
    out = out * lax.rsqrt(jnp.mean(out * out, axis=-1, keepdims=True) + NORM_EPS) * gain_ref[...]
    o_ref[...] = x_ref[...] + out


def _merge(ys, u_all, w_br, w_out, gain, x2d, *, tm=256):
    t = x2d.shape[0]
    tm = min(tm, t)
    gblk = _U_LAYOUT['merge_gate'][0] // D_MODEL
    yspec = pl.BlockSpec((tm, BR_WIDTH), lambda i: (i, 0))
    gspec = lambda n: pl.BlockSpec((tm, D_MODEL), lambda i: (i, gblk + n))
    return pl.pallas_call(
        _merge_kernel,
        grid=(t // tm,),
        in_specs=[yspec] * 4 + [gspec(n) for n in range(4)] + [
            pl.BlockSpec((N_BRANCH, BR_WIDTH, D_MODEL), lambda i: (0, 0, 0)),
            pl.BlockSpec((D_MODEL, D_MODEL), lambda i: (0, 0)),
            pl.BlockSpec((1, D_MODEL), lambda i: (0, 0)),
            pl.BlockSpec((tm, D_MODEL), lambda i: (i, 0)),
        ],
        out_specs=pl.BlockSpec((tm, D_MODEL), lambda i: (i, 0)),
        out_shape=jax.ShapeDtypeStruct((t, D_MODEL), F32),
        compiler_params=pltpu.CompilerParams(dimension_semantics=("arbitrary",), vmem_limit_bytes=_VMEM_LIMIT),
        name="merge",
    )(*ys, u_all, u_all, u_all, u_all, w_br.astype(_MXU), w_out.astype(_MXU), gain.reshape(1, D_MODEL).astype(F32), x2d)


def _layer(x2d, bsz, s_len, p):
    u_all = _inproj(x2d, p['norm_pre'].astype(F32), p['w_all'])
    ys = (
        _gdn(u_all, bsz, s_len, p['conv_a'], p['a_log_a'], p['dt_bias_a'], p['onorm_a']),
        _gla(u_all, bsz, s_len, p['w_gk'], p['b_gk'], p['onorm_b']),
        _ssd(u_all, bsz, s_len, p['conv_c'], p['conv_bias_c'], p['a_log_c'], p['dt_bias_c'], p['d_skip_c'], p['onorm_c']),
        _nsa(u_all, bsz, s_len, p['cmp_pos_k'], p['cmp_pos_v'], p['w_ck1'], p['w_ck2'], p['w_cv1'], p['w_cv2']),
    )
    return _merge(ys, u_all, p['w_br'], p['w_out'], p['norm_post'], x2d)


def kernel(x, norm_pre, norm_post, w_in, conv_a, a_log_a, dt_bias_a, onorm_a, w_gk, b_gk, onorm_b, conv_c, conv_bias_c, a_log_c, dt_bias_c, d_skip_c, onorm_c, cmp_pos_k, cmp_pos_v, w_ck1, w_ck2, w_cv1, w_cv2, w_br, w_out):
    bsz, s_len, _ = x.shape
    params = dict(norm_pre=norm_pre, norm_post=norm_post, w_all=_layout_weights(w_in).astype(_MXU),
                  conv_a=conv_a, a_log_a=a_log_a,
                  dt_bias_a=dt_bias_a, onorm_a=onorm_a, w_gk=w_gk, b_gk=b_gk, onorm_b=onorm_b, conv_c=conv_c,
                  conv_bias_c=conv_bias_c, a_log_c=a_log_c, dt_bias_c=dt_bias_c, d_skip_c=d_skip_c,
                  onorm_c=onorm_c, cmp_pos_k=cmp_pos_k, cmp_pos_v=cmp_pos_v, w_ck1=w_ck1, w_ck2=w_ck2,
                  w_cv1=w_cv1, w_cv2=w_cv2, w_br=w_br, w_out=w_out)
    x2d = x.reshape(bsz * s_len, D_MODEL)
    for l in range(w_in.shape[0]):
        x2d = _layer(x2d, bsz, s_len, {k: v[l] for k, v in params.items()})
    return x2d.reshape(bsz, s_len, D_MODEL)
```

```python
import functools

import jax
import jax.numpy as jnp
from jax import lax
from jax.experimental import pallas as pl
from jax.experimental.pallas import tpu as pltpu

D_MODEL = 1024
N_BRANCH = 4
BR_WIDTH = 512
CONV_K = 4
NORM_EPS = 1e-6

GDN_HEADS = 4
GDN_DK = 128
GDN_DV = 128
GLA_HEADS = 4
GLA_DK = 64
GLA_DV = 128
GLA_LOWRANK = 16
GLA_GATE_NORMALIZER = 16.0
SSD_HEADS = 8
SSD_HEADDIM = 64
SSD_STATE = 128
SSD_GROUPS = 2
NSA_HEADS = 8
NSA_KV_HEADS = 2
NSA_HEADDIM = 64
CMP_BLOCK = 32
CMP_STRIDE = 16
SEL_BLOCK = 64
SEL_TOPK = 16
WINDOW = 512
FORCED_SCORE = 1e4
CHUNK = 64

F32 = jnp.float32
BF16 = jnp.bfloat16
_MXU = jnp.bfloat16
_VMEM_LIMIT = 56 * 1024 * 1024
_MASKED = -1e30
_UNSELECTED = -1e30
_NSA_TQ, _NSA_TK = 256, 256

_U_LAYOUT = {}
_off = 0
for _name, _w in (
    ('gdn_q', 512), ('gdn_k', 512), ('gdn_v', 512), ('gdn_z', 512),
    ('gla_q', 256), ('gla_k', 256), ('gla_v', 512), ('gla_z', 512),
    ('ssd_x', 512), ('ssd_b', 256), ('ssd_c', 256), ('ssd_z', 512),
    ('nsa_q', 512), ('nsa_kc', 128), ('nsa_vc', 128), ('nsa_ks', 128), ('nsa_vs', 128),
    ('nsa_kw', 128), ('nsa_vw', 128), ('nsa_z', 512),
    ('small', 128), ('pad', 128), ('merge_gate', 4096),
):
    _U_LAYOUT[_name] = (_off, _w)
    _off += _w
U_WIDTH = _off
SM_BETA, SM_A, SM_GK, SM_DT, SM_GATE = 0, 4, 8, 24, 32

_IN_SPLITS = (
    ('gdn_q', 512), ('gdn_k', 512), ('gdn_v', 512), ('gdn_beta', 4), ('gdn_a', 4), ('gdn_z', 512),
    ('gla_q', 256), ('gla_k', 256), ('gla_v', 512), ('gla_gk', 16), ('gla_z', 512),
    ('ssd_x', 512), ('ssd_b', 256), ('ssd_c', 256), ('ssd_dt', 8), ('ssd_z', 512),
    ('nsa_q', 512), ('nsa_kc', 128), ('nsa_vc', 128), ('nsa_ks', 128), ('nsa_vs', 128),
    ('nsa_kw', 128), ('nsa_vw', 128), ('nsa_gate', 24), ('nsa_z', 512),
    ('merge_gate', 4096),
)
_SMALL_COL = {'gdn_beta': SM_BETA, 'gdn_a': SM_A, 'gla_gk': SM_GK, 'ssd_dt': SM_DT, 'nsa_gate': SM_GATE}


def _layout_weights(w_in):
    src, start = {}, 0
    for name, width in _IN_SPLITS:
        src[name] = (start, width)
        start += width
    take = lambda name: lax.slice_in_dim(w_in, src[name][0], src[name][0] + src[name][1], axis=w_in.ndim - 1)
    zeros = lambda n: jnp.zeros(w_in.shape[:-1] + (n,), w_in.dtype)
    pieces = []
    for name, (_, width) in _U_LAYOUT.items():
        if name == 'small':
            pos = 0
            for sname, col in sorted(_SMALL_COL.items(), key=lambda kv: kv[1]):
                if col > pos:
                    pieces.append(zeros(col - pos))
                pieces.append(take(sname))
                pos = col + src[sname][1]
            pieces.append(zeros(width - pos))
        elif name == 'pad':
            pieces.append(zeros(width))
        else:
            pieces.append(take(name))
    return jnp.concatenate(pieces, axis=-1)


def _mm(a, b):
    return jnp.dot(a.astype(_MXU), b.astype(_MXU), preferred_element_type=F32)


def _mm_nt(a, b):
    return lax.dot_general(a.astype(_MXU), b.astype(_MXU), (((1,), (1,)), ((), ())),
                           preferred_element_type=F32)


def _mm_tn(a, b):
    return lax.dot_general(a.astype(_MXU), b.astype(_MXU), (((0,), (0,)), ((), ())),
                           preferred_element_type=F32)


def _split_bf16(a):
    hi = a.astype(BF16)
    return hi, (a - hi.astype(F32)).astype(BF16)


def _mm3(a, b):
    ah, al = _split_bf16(a)
    bh, bl = _split_bf16(b)
    d = lambda x, y: jnp.dot(x, y, preferred_element_type=F32)
    return d(ah, bh) + d(ah, bl) + d(al, bh)


def _mm01(a01, b, b_contract=0):
    bh = b.astype(BF16)
    bm, bl = _split_bf16(b - bh.astype(F32))
    a = a01.astype(BF16)
    d = lambda y: lax.dot_general(a, y, (((1,), (b_contract,)), ((), ())), preferred_element_type=F32)
    return d(bh) + d(bm) + d(bl)


def _mm10(a, b01):
    ah = a.astype(BF16)
    am, al = _split_bf16(a - ah.astype(F32))
    b = b01.astype(BF16)
    d = lambda x: jnp.dot(x, b, preferred_element_type=F32)
    return d(ah) + d(am) + d(al)


def _silu(x):
    return x * jax.nn.sigmoid(x)


def _softplus(x):
    return jnp.maximum(x, 0.0) + jnp.log(1.0 + jnp.exp(-jnp.abs(x)))


def _iota2(shape, axis):
    return lax.broadcasted_iota(jnp.int32, shape, axis)


def _causal_conv_silu(xbuf, r0, col, width, w_ref, bias=None, rows=CHUNK):
    xx = xbuf[pl.ds(r0, rows + 8), col:col + width]
    w = w_ref[:, col:col + width]
    acc = xx * w[CONV_K - 1:CONV_K, :]
    for k in range(1, CONV_K):
        acc = acc + pltpu.roll(xx, k, 0) * w[CONV_K - 1 - k:CONV_K - k, :]
    y = acc[8:, :]
    if bias is not None:
        y = y + bias
    return _silu(y)


def _inproj_kernel(x_ref, g_ref, w_ref, o_ref, h_ref):
    @pl.when(pl.program_id(1) == 0)
    def _():
        x = x_ref[...]
        ms = jnp.mean(x * x, axis=-1, keepdims=True)
        h_ref[...] = (x * lax.rsqrt(ms + NORM_EPS) * g_ref[...]).astype(h_ref.dtype)

    o_ref[...] = jnp.dot(h_ref[...], w_ref[...], preferred_element_type=F32)


def _inproj(x2d, gain, w_all, *, tm=1024, tn=1408):
    t = x2d.shape[0]
    tm = min(tm, t)
    return pl.pallas_call(
        _inproj_kernel,
        grid=(t // tm, U_WIDTH // tn),
        in_specs=[
            pl.BlockSpec((tm, D_MODEL), lambda i, j: (i, 0)),
            pl.BlockSpec((1, D_MODEL), lambda i, j: (0, 0)),
            pl.BlockSpec((D_MODEL, tn), lambda i, j: (0, j)),
        ],
        out_specs=pl.BlockSpec((tm, tn), lambda i, j: (i, j)),
        out_shape=jax.ShapeDtypeStruct((t, U_WIDTH), F32),
        scratch_shapes=[pltpu.VMEM((tm, D_MODEL), _MXU)],
        compiler_params=pltpu.CompilerParams(
            dimension_semantics=("arbitrary", "arbitrary"), vmem_limit_bytes=_VMEM_LIMIT),
        name="inproj",
    )(x2d, gain.reshape(1, D_MODEL), w_all)


def _gdn_kernel(qkv_ref, z_ref, sm_ref, cw_ref, alog_ref, dtb_ref, onorm_ref, y_ref, xbuf, state, *, tc):
    c = CHUNK

    @pl.when(pl.program_id(1) == 0)
    def _():
        state[...] = jnp.zeros_like(state)
        xbuf[0:8, :] = jnp.zeros((8, xbuf.shape[1]), F32)

    xbuf[8:8 + tc, :] = qkv_ref[...]

    nch = tc // c
    ri, ci = _iota2((tc, tc), 0), _iota2((tc, tc), 1)
    same = (ri // c) == (ci // c)
    tril = same & (ri >= ci)
    stril = same & (ri > ci)
    tril_f = tril.astype(F32)
    eye_f = (ri == ci).astype(F32)
    neg_a = -jnp.exp(alog_ref[...])
    sm = sm_ref[...]
    beta_all = jax.nn.sigmoid(sm)
    g_all = neg_a * _softplus(sm + dtb_ref[...])
    onorm = onorm_ref[...]

    heads = range(GDN_HEADS)
    qs, ks, vbs, kbs, gcs, decays = [], [], [], [], [], []
    for h in heads:
        q = _causal_conv_silu(xbuf, 0, h * GDN_DK, GDN_DK, cw_ref, rows=tc)
        k = _causal_conv_silu(xbuf, 0, 512 + h * GDN_DK, GDN_DK, cw_ref, rows=tc)
        v = _causal_conv_silu(xbuf, 0, 1024 + h * GDN_DV, GDN_DV, cw_ref, rows=tc)
        q = q * lax.rsqrt(jnp.sum(q * q, axis=-1, keepdims=True) + NORM_EPS) * (GDN_DK ** -0.5)
        k = k * lax.rsqrt(jnp.sum(k * k, axis=-1, keepdims=True) + NORM_EPS)
        beta = jnp.broadcast_to(beta_all[:, SM_BETA + h:SM_BETA + h + 1], (tc, GDN_DK))
        gb = jnp.broadcast_to(g_all[:, SM_A + h:SM_A + h + 1], (tc, GDN_DK))
        gc = _mm01(tril_f, gb)
        gcw = jnp.concatenate([gc] * (tc // GDN_DK), axis=1)
        qs.append(q)
        ks.append(k)
        vbs.append(v * beta)
        kbs.append(k * beta)
        gcs.append(gc)
        decays.append(jnp.exp(jnp.where(tril, gcw - gcw.T, -jnp.inf)))
    lowers = [jnp.where(stril, _mm_nt(kbs[h], ks[h]) * decays[h], 0.0) for h in heads]
    npows = [-lo for lo in lowers]
    invs = [eye_f + n for n in npows]
    for _ in range(5):
        npows = [_mm(n, n) for n in npows]
        invs = [x + _mm(x, n) for x, n in zip(invs, npows)]
    egcs = [jnp.exp(gc) for gc in gcs]
    rhss = [jnp.concatenate([vbs[h], kbs[h] * egcs[h]], axis=1) for h in heads]
    sols = [_mm(invs[h], rhss[h]) for h in heads]
    resids = [rhss[h] - (sols[h] + _mm3(lowers[h], sols[h])) for h in heads]
    sols = [sols[h] + _mm(invs[h], resids[h]) for h in heads]
    attns = [_mm_nt(qs[h], ks[h]) * decays[h] for h in heads]
    qgs = [qs[h] * egcs[h] for h in heads]
    gc_lasts = [jnp.broadcast_to(gc.reshape(nch, c, GDN_DK)[:, c - 1:c, :], (nch, c, GDN_DK)).reshape(tc, GDN_DK)
                for gc in gcs]
    kds = [ks[h] * jnp.exp(gc_lasts[h] - gcs[h]) for h in heads]
    decs = [jnp.exp(g) for g in gc_lasts]
    sts = [state[h] for h in heads]
    v_news = [[] for _ in heads]
    o_inter = [[] for _ in heads]
    for ck in range(nch):
        rows = slice(ck * c, (ck + 1) * c)
        for h in heads:
            v_new = sols[h][rows, :GDN_DV] - _mm(sols[h][rows, GDN_DV:], sts[h])
            o_inter[h].append(_mm(qgs[h][rows], sts[h]))
            sts[h] = sts[h] * decs[h][ck * c:ck * c + 1, :] + _mm_tn(kds[h][rows], v_new)
            v_news[h].append(v_new)
    for h in heads:
        state[h] = sts[h]
        o = jnp.concatenate(o_inter[h], axis=0) + _mm(attns[h], jnp.concatenate(v_news[h], axis=0))
        o = o * lax.rsqrt(jnp.mean(o * o, axis=-1, keepdims=True) + NORM_EPS) * onorm
        y_ref[:, h * GDN_DV:(h + 1) * GDN_DV] = o * _silu(z_ref[:, h * GDN_DV:(h + 1) * GDN_DV])

    xbuf[0:8, :] = qkv_ref[tc - 8:tc, :]


def _small_row(vec, col):
    return jnp.zeros((1, 128), F32).at[0, col:col + vec.shape[0]].set(vec.astype(F32))


def _gdn(u_all, bsz, s_len, conv_w, a_log, dt_bias, onorm, *, tc=256):
    tc = min(tc, s_len)
    ns = s_len // tc
    sm_blk = _U_LAYOUT['small'][0] // 128
    return pl.pallas_call(
        functools.partial(_gdn_kernel, tc=tc),
        grid=(bsz, ns),
        in_specs=[
            pl.BlockSpec((tc, 1536), lambda b, s: (b * ns + s, 0)),
            pl.BlockSpec((tc, 512), lambda b, s: (b * ns + s, _U_LAYOUT['gdn_z'][0] // 512)),
            pl.BlockSpec((tc, 128), lambda b, s: (b * ns + s, sm_blk)),
            pl.BlockSpec((CONV_K, 1536), lambda b, s: (0, 0)),
            pl.BlockSpec((1, 128), lambda b, s: (0, 0)),
            pl.BlockSpec((1, 128), lambda b, s: (0, 0)),
            pl.BlockSpec((1, GDN_DV), lambda b, s: (0, 0)),
        ],
        out_specs=pl.BlockSpec((tc, 512), lambda b, s: (b * ns + s, 0)),
        out_shape=jax.ShapeDtypeStruct((bsz * s_len, 512), F32),
        scratch_shapes=[pltpu.VMEM((tc + 8, 1536), F32), pltpu.VMEM((GDN_HEADS, GDN_DK, GDN_DV), F32)],
        compiler_params=pltpu.CompilerParams(
            dimension_semantics=("arbitrary", "arbitrary"), vmem_limit_bytes=_VMEM_LIMIT),
        name="gdn",
    )(u_all, u_all, u_all, conv_w.astype(F32), _small_row(a_log, SM_A), _small_row(dt_bias, SM_A),
      onorm.reshape(1, GDN_DV).astype(F32))


def _gla_kernel(qk_ref, v_ref, z_ref, sm_ref, wgk_ref, bgk_ref, onorm_ref, y_ref, state_t, *, tc):
    c = CHUNK

    @pl.when(pl.program_id(1) == 0)
    def _():
        state_t[...] = jnp.zeros_like(state_t)

    nch = tc // c
    nk = GLA_HEADS * GLA_DK
    ri, ci = _iota2((tc, tc), 0), _iota2((tc, tc), 1)
    tril = ((ri // c) == (ci // c)) & (ri >= ci)
    onorm = onorm_ref[...]
    heads = range(GLA_HEADS)

    pre = _mm(sm_ref[...], wgk_ref[...]) + bgk_ref[...]
    gk = (jnp.minimum(pre, 0.0) - jnp.log(1.0 + jnp.exp(-jnp.abs(pre)))) * (1.0 / GLA_GATE_NORMALIZER)
    b = _mm01(tril.astype(F32), gk)
    b3 = b.reshape(nch, c, nk)
    at_row = lambda r: jnp.broadcast_to(b3[:, r:r + 1, :], (nch, c, nk)).reshape(tc, nk)
    bref, b_last = at_row(c // 2), at_row(c - 1)
    q = qk_ref[:, 0:nk] * (GLA_DK ** -0.5)
    k = qk_ref[:, nk:2 * nk]
    q_in = q * jnp.exp(b - bref)
    k_in = k * jnp.exp(bref - b)
    qg = q * jnp.exp(b)
    kd = k * jnp.exp(b_last - b)
    dec = jnp.exp(b_last)
    hs = lambda x, h: x[:, h * GLA_DK:(h + 1) * GLA_DK]
    vs = [v_ref[:, h * GLA_DV:(h + 1) * GLA_DV] for h in heads]
    a_intra = [jnp.where(tril, _mm_nt(hs(q_in, h), hs(k_in, h)), 0.0) for h in heads]
    o_intra = [_mm(a_intra[h], vs[h]) for h in heads]
    contrib = [[_mm_tn(vs[h][ck * c:(ck + 1) * c], hs(kd, h)[ck * c:(ck + 1) * c]) for ck in range(nch)]
               for h in heads]
    for h in heads:
        st = state_t[h]
        o_inter = []
        for ck in range(nch):
            o_inter.append(_mm_nt(hs(qg, h)[ck * c:(ck + 1) * c], st))
            st = st * hs(dec, h)[ck * c:ck * c + 1, :] + contrib[h][ck]
        state_t[h] = st
        o = o_intra[h] + jnp.concatenate(o_inter, axis=0)
        o = o * lax.rsqrt(jnp.mean(o * o, axis=-1, keepdims=True) + NORM_EPS) * onorm
        y_ref[:, h * GLA_DV:(h + 1) * GLA_DV] = o * _silu(z_ref[:, h * GLA_DV:(h + 1) * GLA_DV])


def _gla(u_all, bsz, s_len, w_gk, b_gk, onorm, *, tc=256):
    tc = min(tc, s_len)
    ns = s_len // tc
    sm_blk = _U_LAYOUT['small'][0] // 128
    w_pad = jnp.zeros((128, GLA_HEADS * GLA_DK), F32).at[SM_GK:SM_GK + GLA_LOWRANK].set(w_gk.astype(F32))
    row = lambda b, s: b * ns + s
    return pl.pallas_call(
        functools.partial(_gla_kernel, tc=tc),
        grid=(bsz, ns),
        in_specs=[
            pl.BlockSpec((tc, 512), lambda b, s: (row(b, s), _U_LAYOUT['gla_q'][0] // 512)),
            pl.BlockSpec((tc, 512), lambda b, s: (row(b, s), _U_LAYOUT['gla_v'][0] // 512)),
            pl.BlockSpec((tc, 512), lambda b, s: (row(b, s), _U_LAYOUT['gla_z'][0] // 512)),
            pl.BlockSpec((tc, 128), lambda b, s: (row(b, s), sm_blk)),
            pl.BlockSpec((128, 256), lambda b, s: (0, 0)),
            pl.BlockSpec((1, 256), lambda b, s: (0, 0)),
            pl.BlockSpec((1, GLA_DV), lambda b, s: (0, 0)),
        ],
        out_specs=pl.BlockSpec((tc, 512), lambda b, s: (row(b, s), 0)),
        out_shape=jax.ShapeDtypeStruct((bsz * s_len, 512), F32),
        scratch_shapes=[pltpu.VMEM((GLA_HEADS, GLA_DV, GLA_DK), F32)],
        compiler_params=pltpu.CompilerParams(
            dimension_semantics=("arbitrary", "arbitrary"), vmem_limit_bytes=_VMEM_LIMIT),
        name="gla",
    )(u_all, u_all, u_all, u_all, w_pad, b_gk.reshape(1, -1).astype(F32), onorm.reshape(1, GLA_DV).astype(F32))


def _ssd_kernel(x_ref, bc_ref, z_ref, sm_ref, cw_ref, cb_ref, alog_ref, dtb_ref, dskip_ref, onorm_ref,
                y_ref, xbuf, state, *, tc):
    c = CHUNK
    hg = SSD_HEADS // SSD_GROUPS

    @pl.when(pl.program_id(1) == 0)
    def _():
        state[...] = jnp.zeros_like(state)
        xbuf[0:8, :] = jnp.zeros((8, xbuf.shape[1]), F32)

    xbuf[8:8 + tc, 0:512] = x_ref[...]
    xbuf[8:8 + tc, 512:1024] = bc_ref[...]

    nch = tc // c
    inner = SSD_HEADS * SSD_HEADDIM
    ri, ci = _iota2((tc, tc), 0), _iota2((tc, tc), 1)
    tril = ((ri // c) == (ci // c)) & (ri >= ci)
    neg_a = -jnp.exp(alog_ref[...])
    cbias = cb_ref[...]
    heads = range(SSD_HEADS)

    dt_all = _softplus(sm_ref[...] + dtb_ref[...])
    acs_all = _mm01(tril.astype(F32), dt_all * neg_a)
    acs_last_all = jnp.broadcast_to(acs_all.reshape(nch, c, 128)[:, c - 1:c, :], (nch, c, 128)).reshape(tc, 128)
    spread_m = (_iota2((128, inner), 0) == SM_DT + _iota2((128, inner), 1) // SSD_HEADDIM).astype(F32)
    spread = lambda cols: _mm10(cols, spread_m)
    xs = _causal_conv_silu(xbuf, 0, 0, inner, cw_ref, cbias[:, 0:inner], rows=tc)
    xdt = xs * spread(dt_all)
    xdec = xdt * spread(jnp.exp(acs_last_all - acs_all))
    bms, cms, cbs = [], [], []
    for g in range(SSD_GROUPS):
        lo = inner + g * SSD_STATE
        bms.append(_causal_conv_silu(xbuf, 0, lo, SSD_STATE, cw_ref, cbias[:, lo:lo + SSD_STATE], rows=tc))
        lo = inner + SSD_GROUPS * SSD_STATE + g * SSD_STATE
        cms.append(_causal_conv_silu(xbuf, 0, lo, SSD_STATE, cw_ref, cbias[:, lo:lo + SSD_STATE], rows=tc))
        cbs.append(_mm_nt(cms[g], bms[g]))
    hl = lambda x, hh: x[:, hh * SSD_HEADDIM:(hh + 1) * SSD_HEADDIM]
    lmats = []
    for hh in heads:
        acs_b = jnp.broadcast_to(acs_all[:, SM_DT + hh:SM_DT + hh + 1], (tc, tc))
        lmats.append(jnp.exp(jnp.where(tril, acs_b - acs_b.T, -jnp.inf)))
    y_diag = [_mm(cbs[hh // hg] * lmats[hh], hl(xdt, hh)) for hh in heads]
    contrib = [[_mm_tn(hl(xdec, hh)[ck * c:(ck + 1) * c], bms[hh // hg][ck * c:(ck + 1) * c]) for ck in range(nch)]
               for hh in heads]
    cdec = jnp.exp(acs_last_all)
    y_off = []
    for hh in heads:
        st = state[hh]
        parts = []
        for ck in range(nch):
            parts.append(_mm_nt(cms[hh // hg][ck * c:(ck + 1) * c], st))
            st = st * cdec[ck * c:ck * c + 1, SM_DT + hh:SM_DT + hh + 1] + contrib[hh][ck]
        state[hh] = st
        y_off.append(jnp.concatenate(parts, axis=0))
    y = (jnp.concatenate(y_diag, axis=1) + jnp.concatenate(y_off, axis=1) * spread(jnp.exp(acs_all))
         + xs * dskip_ref[...])
    y = y * _silu(z_ref[...])
    y_ref[...] = y * lax.rsqrt(jnp.mean(y * y, axis=-1, keepdims=True) + NORM_EPS) * onorm_ref[...]
    xbuf[0:8, 0:512] = x_ref[tc - 8:tc, :]
    xbuf[0:8, 512:1024] = bc_ref[tc - 8:tc, :]


def _ssd(u_all, bsz, s_len, conv_w, conv_b, a_log, dt_bias, d_skip, onorm, *, tc=256):
    tc = min(tc, s_len)
    ns = s_len // tc
    sm_blk = _U_LAYOUT['small'][0] // 128
    row = lambda b, s: b * ns + s
    const = lambda b, s: (0, 0)
    return pl.pallas_call(
        functools.partial(_ssd_kernel, tc=tc),
        grid=(bsz, ns),
        in_specs=[
            pl.BlockSpec((tc, 512), lambda b, s: (row(b, s), _U_LAYOUT['ssd_x'][0] // 512)),
            pl.BlockSpec((tc, 512), lambda b, s: (row(b, s), _U_LAYOUT['ssd_b'][0] // 512)),
            pl.BlockSpec((tc, 512), lambda b, s: (row(b, s), _U_LAYOUT['ssd_z'][0] // 512)),
            pl.BlockSpec((tc, 128), lambda b, s: (row(b, s), sm_blk)),
            pl.BlockSpec((CONV_K, 1024), const),
            pl.BlockSpec((1, 1024), const),
            pl.BlockSpec((1, 128), const),
            pl.BlockSpec((1, 128), const),
            pl.BlockSpec((1, 512), const),
            pl.BlockSpec((1, 512), const),
        ],
        out_specs=pl.BlockSpec((tc, 512), lambda b, s: (row(b, s), 0)),
        out_shape=jax.ShapeDtypeStruct((bsz * s_len, 512), F32),
        scratch_shapes=[pltpu.VMEM((tc + 8, 1024), F32), pltpu.VMEM((SSD_HEADS, SSD_HEADDIM, SSD_STATE), F32)],
        compiler_params=pltpu.CompilerParams(
            dimension_semantics=("arbitrary", "arbitrary"), vmem_limit_bytes=_VMEM_LIMIT),
        name="ssd",
    )(u_all, u_all, u_all, u_all, conv_w.astype(F32), conv_b.reshape(1, -1).astype(F32),
      _small_row(a_log, SM_DT), _small_row(dt_bias, SM_DT),
      jnp.repeat(d_skip.astype(F32), SSD_HEADDIM).reshape(1, 512), onorm.reshape(1, 512).astype(F32))


def _compress_kernel(kc_ref, vc_ref, pos_ref, w1_ref, w2_ref, o_ref, *, nsub):
    for kv, ref in enumerate((kc_ref, vc_ref)):
        e = jnp.zeros((nsub, 128), F32)
        f = jnp.zeros((nsub, 128), F32)
        for r in range(CMP_STRIDE):
            x = ref[pl.ds(r, nsub, stride=CMP_STRIDE), :]
            e = e + _mm(x + pos_ref[kv, r], w1_ref[kv, r])
            f = f + _mm(x + pos_ref[kv, CMP_STRIDE + r], w1_ref[kv, CMP_STRIDE + r])
        pre = e + pltpu.roll(f, nsub - 1, 0)
        o_ref[kv, 0] = _mm(_silu(pre), w2_ref[kv])


def _per_group(w):
    eye = jnp.eye(NSA_KV_HEADS, dtype=w.dtype)
    out = jnp.einsum('gh,...ij->...gihj', eye, w)
    return out.reshape(w.shape[:-2] + (NSA_KV_HEADS * w.shape[-2], NSA_KV_HEADS * w.shape[-1]))


def _compress(u_all, bsz, s_len, pos, w1, w2):
    hd = NSA_HEADDIM
    nsub = s_len // CMP_STRIDE
    w1_bd = _per_group(w1.reshape(2, CMP_BLOCK, hd, hd)).astype(_MXU)
    pos_bd = jnp.tile(pos.astype(F32), (1, 1, NSA_KV_HEADS)).reshape(2, CMP_BLOCK, 1, NSA_KV_HEADS * hd)
    const = lambda n: (lambda b: (0,) * n)
    return pl.pallas_call(
        functools.partial(_compress_kernel, nsub=nsub),
        grid=(bsz,),
        in_specs=[
            pl.BlockSpec((s_len, 128), lambda b: (b, _U_LAYOUT['nsa_kc'][0] // 128)),
            pl.BlockSpec((s_len, 128), lambda b: (b, _U_LAYOUT['nsa_vc'][0] // 128)),
            pl.BlockSpec(pos_bd.shape, const(4)),
            pl.BlockSpec(w1_bd.shape, const(4)),
            pl.BlockSpec((2, 128, 128), const(3)),
        ],
        out_specs=pl.BlockSpec((2, 1, nsub, 128), lambda b: (0, b, 0, 0)),
        out_shape=jax.ShapeDtypeStruct((2, bsz, nsub, 128), F32),
        compiler_params=pltpu.CompilerParams(dimension_semantics=("arbitrary",), vmem_limit_bytes=_VMEM_LIMIT),
        name="nsa_compress",
    )(u_all, u_all, pos_bd, w1_bd, _per_group(w2).astype(_MXU))


def _kvprep_kernel(ks_ref, vs_ref, kw_ref, vw_ref, fs_ref, fw_ref, ksel_ref, vsel_ref, kwin_ref, vwin_ref):
    hd = NSA_HEADDIM
    low = _iota2(ks_ref.shape, 1) < hd
    for k_ref, f_ref, o_ref in ((ks_ref, fs_ref, ksel_ref), (kw_ref, fw_ref, kwin_ref)):
        k = k_ref[...]
        f = f_ref[...]
        o_ref[0] = jnp.where(low, k, f).astype(o_ref.dtype)
        o_ref[1] = jnp.where(low, pltpu.roll(k, hd, 1), f).astype(o_ref.dtype)
    for v_ref, o_ref in ((vs_ref, vsel_ref), (vw_ref, vwin_ref)):
        vt = v_ref[...].T
        o_ref[0, 0] = vt[:hd].astype(o_ref.dtype)
        o_ref[1, 0] = vt[hd:].astype(o_ref.dtype)


def _kvprep(u_all, bsz, s_len, feat_sel, feat_win, *, tk):
    hd, ngrp = NSA_HEADDIM, NSA_KV_HEADS
    nt = s_len // tk
    ucol = lambda name: pl.BlockSpec((tk, 128), lambda b, j: (b * nt + j, _U_LAYOUT[name][0] // 128))
    fspec = pl.BlockSpec((tk, 128), lambda b, j: (j, 0))
    kspec = pl.BlockSpec((ngrp, tk, 128), lambda b, j: (b, j, 0))
    vspec = pl.BlockSpec((ngrp, 1, hd, tk), lambda b, j: (b, j, 0, 0))
    kshape = jax.ShapeDtypeStruct((bsz * ngrp, s_len, 128), _MXU)
    vshape = jax.ShapeDtypeStruct((bsz * ngrp, nt, hd, tk), _MXU)
    return pl.pallas_call(
        _kvprep_kernel,
        grid=(bsz, nt),
        in_specs=[ucol('nsa_ks'), ucol('nsa_vs'), ucol('nsa_kw'), ucol('nsa_vw'), fspec, fspec],
        out_specs=[kspec, vspec, kspec, vspec],
        out_shape=[kshape, vshape, kshape, vshape],
        compiler_params=pltpu.CompilerParams(
            dimension_semantics=("arbitrary", "arbitrary"), vmem_limit_bytes=_VMEM_LIMIT),
        name="nsa_kv_prep",
    )(u_all, u_all, u_all, u_all, feat_sel, feat_win)


def _cmpattn_kernel(slopes_ref, q_ref, kc_ref, vc_ref, ovt_ref, oc_ref, qsel_ref, used_ref, *, tq, nsel, nc, topk):
    hd, ngrp = NSA_HEADDIM, NSA_KV_HEADS
    hg = NSA_HEADS // ngrp
    q0 = pl.program_id(1) * tq
    nsub = kc_ref.shape[2]
    t = q0 + _iota2((tq, nsub), 0)
    ccol = _iota2((tq, nsub), 1)
    cpos = ccol * CMP_STRIDE + (CMP_BLOCK - 1)
    valid = (cpos <= t) & (ccol < nc)
    dist = (t - cpos).astype(F32)
    lane = _iota2((tq, 2 * hd), 1)
    kcs = [kc_ref[0, 0][:, g * hd:(g + 1) * hd] for g in range(ngrp)]
    vcs = [vc_ref[0, 0][:, g * hd:(g + 1) * hd] for g in range(ngrp)]
    heads = [(g, h) for g in range(ngrp) for h in range(hg)]

    def q_slab(idx):
        slab = q_ref[:, (idx // 2) * 2 * hd:(idx // 2 + 1) * 2 * hd] * (hd ** -0.5)
        return slab if idx % 2 == 0 else pltpu.roll(slab, hd, 1)

    slabs = [q_slab(g * hg + h) for g, h in heads]
    ss = [_mm_nt(slabs[i][:, :hd], kcs[g]) - slopes_ref[g * hg + h] * dist for i, (g, h) in enumerate(heads)]
    ss = [jnp.where(valid, s, -jnp.inf) for s in ss]
    ms = [jnp.max(s, axis=-1, keepdims=True) for s in ss]
    es = [jnp.exp(s - jnp.where(m > -jnp.inf, m, 0.0)) for s, m in zip(ss, ms)]
    ps = [e / jnp.maximum(jnp.sum(e, axis=-1, keepdims=True), 1e-30) for e in es]
    for i, (g, h) in enumerate(heads):
        oc_ref[g, :, h * hd:(h + 1) * hd] = _mm(ps[i], vcs[g])
    jj = _iota2((nsel, tq), 0)
    cur = (q0 + _iota2((nsel, tq), 1)) // SEL_BLOCK
    forced = (jj == 0) | (jj == cur) | (jj == cur - 1)
    for g in range(ngrp):
        psum = ps[g * hg]
        for h in range(1, hg):
            psum = psum + ps[g * hg + h]
        imp = _mm01(ovt_ref[...], psum, b_contract=1)
        imp = jnp.where(jj <= cur, jnp.where(forced, FORCED_SCORE, imp), -1.0)
        rank = jnp.zeros((nsel, tq), jnp.int32)
        for j2 in range(nsel):
            row = imp[j2:j2 + 1, :]
            before = (row > imp) | ((row == imp) & (jj > j2))
            rank = rank + before.astype(jnp.int32)
        picked = (rank < topk) & (imp >= 0.0)
        npicked = picked.astype(F32) if g == 0 else npicked + picked.astype(F32)
        bias_t = jnp.where(picked, 0.0, _UNSELECTED)
        pieces = [jnp.zeros((hd, tq), F32), bias_t]
        if nsel < hd:
            pieces.append(jnp.zeros((hd - nsel, tq), F32))
        bias = jnp.concatenate(pieces, axis=0).T
        for h in range(hg):
            feat = bias + jnp.where(lane == hd, slopes_ref[g * hg + h], 0.0)
            qsel_ref[g, h] = jnp.where(lane < hd, slabs[g * hg + h], feat).astype(qsel_ref.dtype)
    nkt = used_ref.shape[2]
    in_tile = (_iota2((nkt, nsel), 0) == _iota2((nkt, nsel), 1) // (nsel // nkt)).astype(F32)
    per_tile = jnp.sum(_mm(in_tile, npicked), axis=-1, keepdims=True)
    used_ref[0, 0] = jnp.broadcast_to(per_tile, (nkt, 128))


def _cmpattn(slopes, u_all, cmp, ovt, bsz, s_len, *, nc, tq, tk):
    hd, ngrp = NSA_HEADDIM, NSA_KV_HEADS
    hg = NSA_HEADS // ngrp
    nkt = s_len // tk
    nq = s_len // tq
    nsel = s_len // SEL_BLOCK
    nsub = cmp.shape[2]
    qw = NSA_HEADS * hd
    return pl.pallas_call(
        functools.partial(_cmpattn_kernel, tq=tq, nsel=nsel, nc=nc, topk=min(SEL_TOPK, nsel)),
        grid=(bsz, nq),
        in_specs=[
            pl.BlockSpec(memory_space=pltpu.SMEM),
            pl.BlockSpec((tq, qw), lambda i, j: (i * nq + j, _U_LAYOUT['nsa_q'][0] // qw)),
            pl.BlockSpec((1, 1, nsub, ngrp * hd), lambda i, j: (0, i, 0, 0)),
            pl.BlockSpec((1, 1, nsub, ngrp * hd), lambda i, j: (1, i, 0, 0)),
            pl.BlockSpec((nsel, nsub), lambda i, j: (0, 0)),
        ],
        out_specs=[
            pl.BlockSpec((ngrp, tq, hg * hd), lambda i, j: (i, j, 0)),
            pl.BlockSpec((ngrp, hg, tq, 2 * hd), lambda i, j: (i, 0, j, 0)),
            pl.BlockSpec((1, 1, nkt, 128), lambda i, j: (i, j, 0, 0)),
        ],
        out_shape=[
            jax.ShapeDtypeStruct((bsz * ngrp, s_len, hg * hd), F32),
            jax.ShapeDtypeStruct((bsz * ngrp, hg, s_len, 2 * hd), _MXU),
            jax.ShapeDtypeStruct((bsz, nq, nkt, 128), F32),
        ],
        compiler_params=pltpu.CompilerParams(
            dimension_semantics=("arbitrary", "arbitrary"), vmem_limit_bytes=_VMEM_LIMIT),
        name="nsa_cmp_attn",
    )(slopes, u_all, cmp, cmp, ovt)


def _flash_tiles(q_alls, k_ref, vt_ref, slope_rows, q0, spans, *, tq, tk, hg, window, needed=None):
    rel = _iota2((tk, tq), 1) - _iota2((tk, tq), 0)
    ngrp = len(q_alls)

    def step(masked, kt, carries):
        k0 = pl.multiple_of(kt * tk, tk)
        if masked:
            d = rel + (q0 - k0)
            ok = d >= 0
            if window is not None:
                ok = ok & (d < window)
            mb = jnp.where(ok, 0.0, _MASKED)
            mb = jnp.concatenate([mb] * hg, axis=1)
        grps = range(ngrp)
        sts = [_mm_nt(k_ref[g, pl.ds(k0, tk), :], q_alls[g]) for g in grps]
        if masked:
            sts = [st + mb for st in sts]
        offs = [slope_rows[g] * k0.astype(F32) for g in grps]
        m_news = [jnp.maximum(carries[g][0], jnp.max(sts[g], axis=0, keepdims=True) + offs[g]) for g in grps]
        ps = [jnp.exp(sts[g] - (m_news[g] - offs[g])) for g in grps]
        alphas = [jnp.exp(carries[g][0] - m_news[g]) for g in grps]
        ls = [alphas[g] * carries[g][1] + jnp.sum(ps[g], axis=0, keepdims=True) for g in grps]
        accs = [alphas[g] * carries[g][2] + _mm(vt_ref[g, kt], ps[g]) for g in grps]
        return tuple((m_news[g], ls[g], accs[g]) for g in grps)

    def pair_step(i, carries):
        kts = (a + 2 * i, a + 2 * i + 1)
        k0s = [pl.multiple_of(kt * tk, tk) for kt in kts]
        grps = range(ngrp)
        sts = [[_mm_nt(k_ref[g, pl.ds(k0, tk), :], q_alls[g]) for g in grps] for k0 in k0s]
        offs = [[slope_rows[g] * k0.astype(F32) for g in grps] for k0 in k0s]
        m_news = [jnp.maximum(carries[g][0],
                              jnp.maximum(jnp.max(sts[0][g], axis=0, keepdims=True) + offs[0][g],
                                          jnp.max(sts[1][g], axis=0, keepdims=True) + offs[1][g])) for g in grps]
        ps = [[jnp.exp(sts[t][g] - (m_news[g] - offs[t][g])) for g in grps] for t in range(2)]
        alphas = [jnp.exp(carries[g][0] - m_news[g]) for g in grps]
        ls = [alphas[g] * carries[g][1] + jnp.sum(ps[0][g], axis=0, keepdims=True)
              + jnp.sum(ps[1][g], axis=0, keepdims=True) for g in grps]
        accs = [alphas[g] * carries[g][2] + _mm(vt_ref[g, kts[0]], ps[0][g]) + _mm(vt_ref[g, kts[1]], ps[1][g])
                for g in grps]
        return tuple((m_news[g], ls[g], accs[g]) for g in grps)

    rows = hg * tq
    init = (jnp.full((1, rows), _MASKED, F32), jnp.zeros((1, rows), F32), jnp.zeros((NSA_HEADDIM, rows), F32))
    carries = (init,) * ngrp
    lo, a, b, hi = spans
    plain_step = functools.partial(step, False)
    if needed is None:
        pair_body, single_body = pair_step, plain_step
    else:
        def pair_body(i, c):
            want = needed(a + 2 * i) | needed(a + 2 * i + 1)
            return lax.cond(want, lambda cc: pair_step(i, cc), lambda cc: cc, c)

        def single_body(kt, c):
            return lax.cond(needed(kt), lambda cc: plain_step(kt, cc), lambda cc: cc, c)

    carries = lax.fori_loop(lo, a, functools.partial(step, True), carries)
    npairs = (b - a) // 2
    carries = lax.fori_loop(0, npairs, pair_body, carries)
    carries = lax.fori_loop(a + 2 * npairs, b, single_body, carries)
    carries = lax.fori_loop(b, hi, functools.partial(step, True), carries)
    return [acc / l for _, l, acc in carries]


def _slope_row(slopes_ref, g, hg, tq):
    return jnp.concatenate([jnp.full((1, tq), slopes_ref[g * hg + h], F32) for h in range(hg)], axis=1)


def _heads_to_lanes(ot, hg, tq):
    pairs = []
    for h in range(0, hg, 2):
        two = jnp.concatenate([ot[:, h * tq:(h + 1) * tq], ot[:, (h + 1) * tq:(h + 2) * tq]], axis=0)
        pairs.append(two.T)
    return jnp.concatenate(pairs, axis=1)


def _selattn_kernel(slopes_ref, used_ref, q_ref, k_ref, vt_ref, o_ref, *, tq, tk, nkt):
    ngrp, hg = NSA_KV_HEADS, NSA_HEADS // NSA_KV_HEADS
    q0 = pl.program_id(1) * tq
    full_hi = (q0 + 1) // tk
    hi = (q0 + tq - 1) // tk + 1
    q_alls = [q_ref[g].reshape(hg * tq, q_ref.shape[-1]) for g in range(ngrp)]
    slope_rows = [_slope_row(slopes_ref, g, hg, tq) for g in range(ngrp)]
    base = (pl.program_id(0) * pl.num_programs(1) + pl.program_id(1)) * nkt
    ots = _flash_tiles(q_alls, k_ref, vt_ref, slope_rows, q0, (0, 0, full_hi, hi),
                       tq=tq, tk=tk, hg=hg, window=None, needed=lambda kt: used_ref[base + kt] > 0)
    for g in range(ngrp):
        o_ref[g] = _heads_to_lanes(ots[g], hg, tq)


def _selattn(slopes, used, qsel, ksel, vsel_t, *, tq, tk):
    bg, hg, s_len, wid = qsel.shape
    hd, ngrp = NSA_HEADDIM, NSA_KV_HEADS
    return pl.pallas_call(
        functools.partial(_selattn_kernel, tq=tq, tk=tk, nkt=s_len // tk),
        grid=(bg // ngrp, s_len // tq),
        in_specs=[
            pl.BlockSpec(memory_space=pltpu.SMEM),
            pl.BlockSpec(memory_space=pltpu.SMEM),
            pl.BlockSpec((ngrp, hg, tq, wid), lambda i, j: (i, 0, j, 0)),
            pl.BlockSpec((ngrp, s_len, wid), lambda i, j: (i, 0, 0)),
            pl.BlockSpec((ngrp, s_len // tk, hd, tk), lambda i, j: (i, 0, 0, 0)),
        ],
        out_specs=pl.BlockSpec((ngrp, tq, hg * hd), lambda i, j: (i, j, 0)),
        out_shape=jax.ShapeDtypeStruct((bg, s_len, hg * hd), F32),
        compiler_params=pltpu.CompilerParams(
            dimension_semantics=("arbitrary", "arbitrary"), vmem_limit_bytes=_VMEM_LIMIT),
        name="nsa_sel_attn",
    )(slopes, used, qsel, ksel, vsel_t)


def _winattn_kernel(slopes_ref, q_ref, k_ref, vt_ref, oc_ref, os_ref, gate_ref, gexp_ref, z0_ref, z1_ref, y_ref,
                    *, tq, tk):
    ngrp, hg = NSA_KV_HEADS, NSA_HEADS // NSA_KV_HEADS
    gw = hg * NSA_HEADDIM
    q0 = pl.program_id(1) * tq
    lo = jnp.maximum(q0 - (WINDOW - 1), 0) // tk
    full_hi = (q0 + 1) // tk
    full_lo = jnp.minimum(jnp.maximum(q0 + tq - WINDOW + tk - 1, 0) // tk, full_hi)
    hi = (q0 + tq - 1) // tk + 1
    q_alls = [q_ref[g].reshape(hg * tq, q_ref.shape[-1]) for g in range(ngrp)]
    slope_rows = [_slope_row(slopes_ref, g, hg, tq) for g in range(ngrp)]
    ots = _flash_tiles(q_alls, k_ref, vt_ref, slope_rows, q0, (lo, full_lo, full_hi, hi),
                       tq=tq, tk=tk, hg=hg, window=WINDOW)
    gates = jax.nn.sigmoid(gate_ref[...])
    for g, z_ref in enumerate((z0_ref, z1_ref)):
        o_w = _heads_to_lanes(ots[g], hg, tq)
        spread = lambda c: _mm10(gates, gexp_ref[g, c])
        o = spread(0) * oc_ref[g] + spread(1) * os_ref[g] + spread(2) * o_w
        y_ref[:, g * gw:(g + 1) * gw] = o * _silu(z_ref[...])


def _winattn(slopes, q_pad, kwin, vwin_t, o_c, o_s, u_all, *, tq, tk):
    bg, hg, s_len, wid = q_pad.shape
    hd, ngrp = NSA_HEADDIM, NSA_KV_HEADS
    nq = s_len // tq
    gw = hg * hd
    zblk = _U_LAYOUT['nsa_z'][0] // gw
    head = jnp.arange(ngrp)[:, None, None, None] * hg + jnp.arange(gw)[None, None, None, :] // hd
    gexp = (jnp.arange(128)[None, None, :, None] == SM_GATE + 3 * head + jnp.arange(3)[None, :, None, None]).astype(F32)
    return pl.pallas_call(
        functools.partial(_winattn_kernel, tq=tq, tk=tk),
        grid=(bg // ngrp, nq),
        in_specs=[
            pl.BlockSpec(memory_space=pltpu.SMEM),
            pl.BlockSpec((ngrp, hg, tq, wid), lambda i, j: (i, 0, j, 0)),
            pl.BlockSpec((ngrp, s_len, wid), lambda i, j: (i, 0, 0)),
            pl.BlockSpec((ngrp, s_len // tk, hd, tk), lambda i, j: (i, 0, 0, 0)),
            pl.BlockSpec((ngrp, tq, gw), lambda i, j: (i, j, 0)),
            pl.BlockSpec((ngrp, tq, gw), lambda i, j: (i, j, 0)),
            pl.BlockSpec((tq, 128), lambda i, j: (i * nq + j, _U_LAYOUT['small'][0] // 128)),
            pl.BlockSpec((ngrp, 3, 128, gw), lambda i, j: (0, 0, 0, 0)),
            pl.BlockSpec((tq, gw), lambda i, j: (i * nq + j, zblk)),
            pl.BlockSpec((tq, gw), lambda i, j: (i * nq + j, zblk + 1)),
        ],
        out_specs=pl.BlockSpec((tq, ngrp * gw), lambda i, j: (i * nq + j, 0)),
        out_shape=jax.ShapeDtypeStruct((bg // ngrp * s_len, ngrp * gw), F32),
        compiler_params=pltpu.CompilerParams(
            dimension_semantics=("arbitrary", "arbitrary"), vmem_limit_bytes=_VMEM_LIMIT),
        name="nsa_win_attn",
    )(slopes, q_pad, kwin, vwin_t, o_c, o_s, u_all, gexp, u_all, u_all)


def _nsa(u_all, bsz, s_len, cmp_pos_k, cmp_pos_v, w_ck1, w_ck2, w_cv1, w_cv2):
    hd = NSA_HEADDIM
    tq, tk = min(_NSA_TQ, s_len), min(_NSA_TK, s_len)
    slopes = 2.0 ** (-8.0 * jnp.arange(1, NSA_HEADS + 1, dtype=F32) / NSA_HEADS)
    nsel = s_len // SEL_BLOCK
    pos = jnp.arange(s_len)
    tile_off = (pos % tk).astype(F32)[:, None]
    onehot = (pos[:, None] // SEL_BLOCK == jnp.arange(1, hd)[None, :]).astype(F32)
    feat_sel = jnp.concatenate([jnp.zeros((s_len, hd), F32), tile_off, onehot], axis=-1)
    feat_win = jnp.concatenate([jnp.zeros((s_len, hd), F32), tile_off, jnp.zeros((s_len, hd - 1), F32)], axis=-1)
    ksel, vsel_t, kwin, vwin_t = _kvprep(u_all, bsz, s_len, feat_sel, feat_win, tk=tk)
    nsub = s_len // CMP_STRIDE
    nc = nsub - CMP_BLOCK // CMP_STRIDE + 1
    cmp = _compress(u_all, bsz, s_len, jnp.stack([cmp_pos_k, cmp_pos_v]), jnp.stack([w_ck1, w_cv1]),
                    jnp.stack([w_ck2, w_cv2]))
    cmp_start = jnp.arange(nsub) * CMP_STRIDE
    sel_start = jnp.arange(nsel) * SEL_BLOCK
    ovt = ((cmp_start[None, :] <= sel_start[:, None] + SEL_BLOCK - 1)
           & (cmp_start[None, :] + CMP_BLOCK - 1 >= sel_start[:, None])
           & (jnp.arange(nsub)[None, :] < nc)).astype(F32)
    o_c, qsel, picked_per_tile = _cmpattn(slopes, u_all, cmp, ovt, bsz, s_len, nc=nc, tq=tq, tk=tk)
    used = (picked_per_tile[..., 0] > 0).astype(jnp.int32).reshape(-1)
    o_s = _selattn(slopes, used, qsel, ksel, vsel_t, tq=tq, tk=tk)
    return _winattn(slopes, qsel, kwin, vwin_t, o_c, o_s, u_all, tq=tq, tk=tk)


def _merge_kernel(y0_ref, y1_ref, y2_ref, y3_ref, g0_ref, g1_ref, g2_ref, g3_ref, wbr_ref, wout_ref, gain_ref,
                  x_ref, o_ref):
    merged = None
    for n, (y_ref, g_ref) in enumerate(((y0_ref, g0_ref), (y1_ref, g1_ref), (y2_ref, g2_ref), (y3_ref, g3_ref))):
        term = jax.nn.sigmoid(g_ref[...]) * jnp.dot(y_ref[...].astype(_MXU), wbr_ref[n], preferred_element_type=F32)
        merged = term if merged is None else merged + term
    out = jnp.dot(merged.astype(_MXU), wout_ref[...], preferred_element_type=F32)
    out = out * lax.rsqrt(jnp.mean(out * out, axis=-1, keepdims=True) + NORM_EPS) * gain_ref[...]
    o_ref[...] = x_ref[...] + out


def _merge(ys, u_all, w_br, w_out, gain, x2d, *, tm=256):
    t = x2d.shape[0]
    tm = min(tm, t)
    gblk = _U_LAYOUT['merge_gate'][0] // D_MODEL
    yspec = pl.BlockSpec((tm, BR_WIDTH), lambda i: (i, 0))
    gspec = lambda n: pl.BlockSpec((tm, D_MODEL), lambda i: (i, gblk + n))
    return pl.pallas_call(
        _merge_kernel,
        grid=(t // tm,),
        in_specs=[yspec] * 4 + [gspec(n) for n in range(4)] + [
            pl.BlockSpec((N_BRANCH, BR_WIDTH, D_MODEL), lambda i: (0, 0, 0)),
            pl.BlockSpec((D_MODEL, D_MODEL), lambda i: (0, 0)),
            pl.BlockSpec((1, D_MODEL), lambda i: (0, 0)),
            pl.BlockSpec((tm, D_MODEL), lambda i: (i, 0)),
        ],
        out_specs=pl.BlockSpec((tm, D_MODEL), lambda i: (i, 0)),
        out_shape=jax.ShapeDtypeStruct((t, D_MODEL), F32),
        compiler_params=pltpu.CompilerParams(dimension_semantics=("arbitrary",), vmem_limit_bytes=_VMEM_LIMIT),
        name="merge",
    )(*ys, u_all, u_all, u_all, u_all, w_br.astype(_MXU), w_out.astype(_MXU), gain.reshape(1, D_MODEL).astype(F32), x2d)


def _layer(x2d, bsz, s_len, p):
    u_all = _inproj(x2d, p['norm_pre'].astype(F32), p['w_all'])
    ys = (
        _gdn(u_all, bsz, s_len, p['conv_a'], p['a_log_a'], p['dt_bias_a'], p['onorm_a']),
        _gla(u_all, bsz, s_len, p['w_gk'], p['b_gk'], p['onorm_b']),
        _ssd(u_all, bsz, s_len, p['conv_c'], p['conv_bias_c'], p['a_log_c'], p['dt_bias_c'], p['d_skip_c'], p['onorm_c']),
        _nsa(u_all, bsz, s_len, p['cmp_pos_k'], p['cmp_pos_v'], p['w_ck1'], p['w_ck2'], p['w_cv1'], p['w_cv2']),
    )
    return _merge(ys, u_all, p['w_br'], p['w_out'], p['norm_post'], x2d)


def kernel(x, norm_pre, norm_post, w_in, conv_a, a_log_a, dt_bias_a, onorm_a, w_gk, b_gk, onorm_b, conv_c, conv_bias_c, a_log_c, dt_bias_c, d_skip_c, onorm_c, cmp_pos_k, cmp_pos_v, w_ck1, w_ck2, w_cv1, w_cv2, w_br, w_out):
    bsz, s_len, _ = x.shape
    params = dict(norm_pre=norm_pre, norm_post=norm_post, w_all=_layout_weights(w_in).astype(_MXU),
                  conv_a=conv_a, a_log_a=a_log_a,
                  dt_bias_a=dt_bias_a, onorm_a=onorm_a, w_gk=w_gk, b_gk=b_gk, onorm_b=onorm_b, conv_c=conv_c,
                  conv_bias_c=conv_bias_c, a_log_c=a_log_c, dt_bias_c=dt_bias_c, d_skip_c=d_skip_c,
                  onorm_c=onorm_c, cmp_pos_k=cmp_pos_k, cmp_pos_v=cmp_pos_v, w_ck1=w_ck1, w_ck2=w_ck2,
                  w_cv1=w_cv1, w_cv2=w_cv2, w_br=w_br, w_out=w_out)
    x2d = x.reshape(bsz * s_len, D_MODEL)
    for l in range(w_in.shape[0]):
        x2d = _layer(x2d, bsz, s_len, {k: v[l] for k, v in params.items()})
    return x2d.reshape(bsz, s_len, D_MODEL)
```

```python
import functools

import jax
import jax.numpy as jnp
from jax import lax
from jax.experimental import pallas as pl
from jax.experimental.pallas import tpu as pltpu

D_MODEL = 1024
N_BRANCH = 4
BR_WIDTH = 512
CONV_K = 4
NORM_EPS = 1e-6

GDN_HEADS = 4
GDN_DK = 128
GDN_DV = 128
GLA_HEADS = 4
GLA_DK = 64
GLA_DV = 128
GLA_LOWRANK = 16
GLA_GATE_NORMALIZER = 16.0
SSD_HEADS = 8
SSD_HEADDIM = 64
SSD_STATE = 128
SSD_GROUPS = 2
NSA_HEADS = 8
NSA_KV_HEADS = 2
NSA_HEADDIM = 64
CMP_BLOCK = 32
CMP_STRIDE = 16
SEL_BLOCK = 64
SEL_TOPK = 16
WINDOW = 512
FORCED_SCORE = 1e4
CHUNK = 64

F32 = jnp.float32
BF16 = jnp.bfloat16
_MXU = jnp.bfloat16
_VMEM_LIMIT = 56 * 1024 * 1024
_MASKED = -1e30
_UNSELECTED = -1e30
_NSA_TQ, _NSA_TK = 256, 256

_U_LAYOUT = {}
_off = 0
for _name, _w in (
    ('gdn_q', 512), ('gdn_k', 512), ('gdn_v', 512), ('gdn_z', 512),
    ('gla_q', 256), ('gla_k', 256), ('gla_v', 512), ('gla_z', 512),
    ('ssd_x', 512), ('ssd_b', 256), ('ssd_c', 256), ('ssd_z', 512),
    ('nsa_q', 512), ('nsa_kc', 128), ('nsa_vc', 128), ('nsa_ks', 128), ('nsa_vs', 128),
    ('nsa_kw', 128), ('nsa_vw', 128), ('nsa_z', 512),
    ('small', 128), ('pad', 128), ('merge_gate', 4096),
):
    _U_LAYOUT[_name] = (_off, _w)
    _off += _w
U_WIDTH = _off
SM_BETA, SM_A, SM_GK, SM_DT, SM_GATE = 0, 4, 8, 24, 32

_IN_SPLITS = (
    ('gdn_q', 512), ('gdn_k', 512), ('gdn_v', 512), ('gdn_beta', 4), ('gdn_a', 4), ('gdn_z', 512),
    ('gla_q', 256), ('gla_k', 256), ('gla_v', 512), ('gla_gk', 16), ('gla_z', 512),
    ('ssd_x', 512), ('ssd_b', 256), ('ssd_c', 256), ('ssd_dt', 8), ('ssd_z', 512),
    ('nsa_q', 512), ('nsa_kc', 128), ('nsa_vc', 128), ('nsa_ks', 128), ('nsa_vs', 128),
    ('nsa_kw', 128), ('nsa_vw', 128), ('nsa_gate', 24), ('nsa_z', 512),
    ('merge_gate', 4096),
)
_SMALL_COL = {'gdn_beta': SM_BETA, 'gdn_a': SM_A, 'gla_gk': SM_GK, 'ssd_dt': SM_DT, 'nsa_gate': SM_GATE}


def _layout_weights(w_in):
    src, start = {}, 0
    for name, width in _IN_SPLITS:
        src[name] = (start, width)
        start += width
    take = lambda name: lax.slice_in_dim(w_in, src[name][0], src[name][0] + src[name][1], axis=w_in.ndim - 1)
    zeros = lambda n: jnp.zeros(w_in.shape[:-1] + (n,), w_in.dtype)
    pieces = []
    for name, (_, width) in _U_LAYOUT.items():
        if name == 'small':
            pos = 0
            for sname, col in sorted(_SMALL_COL.items(), key=lambda kv: kv[1]):
                if col > pos:
                    pieces.append(zeros(col - pos))
                pieces.append(take(sname))
                pos = col + src[sname][1]
            pieces.append(zeros(width - pos))
        elif name == 'pad':
            pieces.append(zeros(width))
        else:
            pieces.append(take(name))
    return jnp.concatenate(pieces, axis=-1)


def _mm(a, b):
    return jnp.dot(a.astype(_MXU), b.astype(_MXU), preferred_element_type=F32)


def _mm_nt(a, b):
    return lax.dot_general(a.astype(_MXU), b.astype(_MXU), (((1,), (1,)), ((), ())),
                           preferred_element_type=F32)


def _mm_tn(a, b):
    return lax.dot_general(a.astype(_MXU), b.astype(_MXU), (((0,), (0,)), ((), ())),
                           preferred_element_type=F32)


def _split_bf16(a):
    hi = a.astype(BF16)
    return hi, (a - hi.astype(F32)).astype(BF16)


def _mm3(a, b):
    ah, al = _split_bf16(a)
    bh, bl = _split_bf16(b)
    d = lambda x, y: jnp.dot(x, y, preferred_element_type=F32)
    return d(ah, bh) + d(ah, bl) + d(al, bh)


def _mm01(a01, b, b_contract=0):
    bh = b.astype(BF16)
    bm, bl = _split_bf16(b - bh.astype(F32))
    a = a01.astype(BF16)
    d = lambda y: lax.dot_general(a, y, (((1,), (b_contract,)), ((), ())), preferred_element_type=F32)
    return d(bh) + d(bm) + d(bl)


def _mm10(a, b01):
    ah = a.astype(BF16)
    am, al = _split_bf16(a - ah.astype(F32))
    b = b01.astype(BF16)
    d = lambda x: jnp.dot(x, b, preferred_element_type=F32)
    return d(ah) + d(am) + d(al)


def _silu(x):
    return x * jax.nn.sigmoid(x)


def _softplus(x):
    return jnp.maximum(x, 0.0) + jnp.log(1.0 + jnp.exp(-jnp.abs(x)))


def _iota2(shape, axis):
    return lax.broadcasted_iota(jnp.int32, shape, axis)


def _causal_conv_silu(xbuf, r0, col, width, w_ref, bias=None, rows=CHUNK):
    xx = xbuf[pl.ds(r0, rows + 8), col:col + width]
    w = w_ref[:, col:col + width]
    acc = xx * w[CONV_K - 1:CONV_K, :]
    for k in range(1, CONV_K):
        acc = acc + pltpu.roll(xx, k, 0) * w[CONV_K - 1 - k:CONV_K - k, :]
    y = acc[8:, :]
    if bias is not None:
        y = y + bias
    return _silu(y)


def _inproj_kernel(x_ref, g_ref, w_ref, o_ref, h_ref):
    @pl.when(pl.program_id(1) == 0)
    def _():
        x = x_ref[...]
        ms = jnp.mean(x * x, axis=-1, keepdims=True)
        h_ref[...] = (x * lax.rsqrt(ms + NORM_EPS) * g_ref[...]).astype(h_ref.dtype)

    o_ref[...] = jnp.dot(h_ref[...], w_ref[...], preferred_element_type=F32)


def _inproj(x2d, gain, w_all, *, tm=2048, tn=1408):
    t = x2d.shape[0]
    tm = min(tm, t)
    return pl.pallas_call(
        _inproj_kernel,
        grid=(t // tm, U_WIDTH // tn),
        in_specs=[
            pl.BlockSpec((tm, D_MODEL), lambda i, j: (i, 0)),
            pl.BlockSpec((1, D_MODEL), lambda i, j: (0, 0)),
            pl.BlockSpec((D_MODEL, tn), lambda i, j: (0, j)),
        ],
        out_specs=pl.BlockSpec((tm, tn), lambda i, j: (i, j)),
        out_shape=jax.ShapeDtypeStruct((t, U_WIDTH), F32),
        scratch_shapes=[pltpu.VMEM((tm, D_MODEL), _MXU)],
        compiler_params=pltpu.CompilerParams(
            dimension_semantics=("arbitrary", "arbitrary"), vmem_limit_bytes=_VMEM_LIMIT),
        name="inproj",
    )(x2d, gain.reshape(1, D_MODEL), w_all)


def _gdn_kernel(qkv_ref, z_ref, sm_ref, cw_ref, alog_ref, dtb_ref, onorm_ref, y_ref, xbuf, state, *, tc):
    c = CHUNK

    @pl.when(pl.program_id(1) == 0)
    def _():
        state[...] = jnp.zeros_like(state)
        xbuf[0:8, :] = jnp.zeros((8, xbuf.shape[1]), F32)

    xbuf[8:8 + tc, :] = qkv_ref[...]

    nch = tc // c
    ri, ci = _iota2((tc, tc), 0), _iota2((tc, tc), 1)
    same = (ri // c) == (ci // c)
    tril = same & (ri >= ci)
    stril = same & (ri > ci)
    tril_f = tril.astype(F32)
    eye_f = (ri == ci).astype(F32)
    neg_a = -jnp.exp(alog_ref[...])
    sm = sm_ref[...]
    beta_all = jax.nn.sigmoid(sm)
    g_all = neg_a * _softplus(sm + dtb_ref[...])
    onorm = onorm_ref[...]

    heads = range(GDN_HEADS)
    qs, ks, vbs, kbs, gcs, decays = [], [], [], [], [], []
    for h in heads:
        q = _causal_conv_silu(xbuf, 0, h * GDN_DK, GDN_DK, cw_ref, rows=tc)
        k = _causal_conv_silu(xbuf, 0, 512 + h * GDN_DK, GDN_DK, cw_ref, rows=tc)
        v = _causal_conv_silu(xbuf, 0, 1024 + h * GDN_DV, GDN_DV, cw_ref, rows=tc)
        q = q * lax.rsqrt(jnp.sum(q * q, axis=-1, keepdims=True) + NORM_EPS) * (GDN_DK ** -0.5)
        k = k * lax.rsqrt(jnp.sum(k * k, axis=-1, keepdims=True) + NORM_EPS)
        beta = jnp.broadcast_to(beta_all[:, SM_BETA + h:SM_BETA + h + 1], (tc, GDN_DK))
        gb = jnp.broadcast_to(g_all[:, SM_A + h:SM_A + h + 1], (tc, GDN_DK))
        gc = _mm01(tril_f, gb)
        gcw = jnp.concatenate([gc] * (tc // GDN_DK), axis=1)
        qs.append(q)
        ks.append(k)
        vbs.append(v * beta)
        kbs.append(k * beta)
        gcs.append(gc)
        decays.append(jnp.exp(jnp.where(tril, gcw - gcw.T, -jnp.inf)))
    lowers = [jnp.where(stril, _mm_nt(kbs[h], ks[h]) * decays[h], 0.0) for h in heads]
    npows = [-lo for lo in lowers]
    invs = [eye_f + n for n in npows]
    for _ in range(5):
        npows = [_mm(n, n) for n in npows]
        invs = [x + _mm(x, n) for x, n in zip(invs, npows)]
    egcs = [jnp.exp(gc) for gc in gcs]
    rhss = [jnp.concatenate([vbs[h], kbs[h] * egcs[h]], axis=1) for h in heads]
    sols = [_mm(invs[h], rhss[h]) for h in heads]
    resids = [rhss[h] - (sols[h] + _mm3(lowers[h], sols[h])) for h in heads]
    sols = [sols[h] + _mm(invs[h], resids[h]) for h in heads]
    attns = [_mm_nt(qs[h], ks[h]) * decays[h] for h in heads]
    qgs = [qs[h] * egcs[h] for h in heads]
    gc_lasts = [jnp.broadcast_to(gc.reshape(nch, c, GDN_DK)[:, c - 1:c, :], (nch, c, GDN_DK)).reshape(tc, GDN_DK)
                for gc in gcs]
    kds = [ks[h] * jnp.exp(gc_lasts[h] - gcs[h]) for h in heads]
    decs = [jnp.exp(g) for g in gc_lasts]
    sts = [state[h] for h in heads]
    v_news = [[] for _ in heads]
    o_inter = [[] for _ in heads]
    for ck in range(nch):
        rows = slice(ck * c, (ck + 1) * c)
        for h in heads:
            v_new = sols[h][rows, :GDN_DV] - _mm(sols[h][rows, GDN_DV:], sts[h])
            o_inter[h].append(_mm(qgs[h][rows], sts[h]))
            sts[h] = sts[h] * decs[h][ck * c:ck * c + 1, :] + _mm_tn(kds[h][rows], v_new)
            v_news[h].append(v_new)
    for h in heads:
        state[h] = sts[h]
        o = jnp.concatenate(o_inter[h], axis=0) + _mm(attns[h], jnp.concatenate(v_news[h], axis=0))
        o = o * lax.rsqrt(jnp.mean(o * o, axis=-1, keepdims=True) + NORM_EPS) * onorm
        y_ref[:, h * GDN_DV:(h + 1) * GDN_DV] = o * _silu(z_ref[:, h * GDN_DV:(h + 1) * GDN_DV])

    xbuf[0:8, :] = qkv_ref[tc - 8:tc, :]


def _small_row(vec, col):
    return jnp.zeros((1, 128), F32).at[0, col:col + vec.shape[0]].set(vec.astype(F32))


def _gdn(u_all, bsz, s_len, conv_w, a_log, dt_bias, onorm, *, tc=256):
    tc = min(tc, s_len)
    ns = s_len // tc
    sm_blk = _U_LAYOUT['small'][0] // 128
    return pl.pallas_call(
        functools.partial(_gdn_kernel, tc=tc),
        grid=(bsz, ns),
        in_specs=[
            pl.BlockSpec((tc, 1536), lambda b, s: (b * ns + s, 0)),
            pl.BlockSpec((tc, 512), lambda b, s: (b * ns + s, _U_LAYOUT['gdn_z'][0] // 512)),
            pl.BlockSpec((tc, 128), lambda b, s: (b * ns + s, sm_blk)),
            pl.BlockSpec((CONV_K, 1536), lambda b, s: (0, 0)),
            pl.BlockSpec((1, 128), lambda b, s: (0, 0)),
            pl.BlockSpec((1, 128), lambda b, s: (0, 0)),
            pl.BlockSpec((1, GDN_DV), lambda b, s: (0, 0)),
        ],
        out_specs=pl.BlockSpec((tc, 512), lambda b, s: (b * ns + s, 0)),
        out_shape=jax.ShapeDtypeStruct((bsz * s_len, 512), F32),
        scratch_shapes=[pltpu.VMEM((tc + 8, 1536), F32), pltpu.VMEM((GDN_HEADS, GDN_DK, GDN_DV), F32)],
        compiler_params=pltpu.CompilerParams(
            dimension_semantics=("arbitrary", "arbitrary"), vmem_limit_bytes=_VMEM_LIMIT),
        name="gdn",
    )(u_all, u_all, u_all, conv_w.astype(F32), _small_row(a_log, SM_A), _small_row(dt_bias, SM_A),
      onorm.reshape(1, GDN_DV).astype(F32))


def _gla_kernel(qk_ref, v_ref, z_ref, sm_ref, wgk_ref, bgk_ref, onorm_ref, y_ref, state_t, *, tc):
    c = CHUNK

    @pl.when(pl.program_id(1) == 0)
    def _():
        state_t[...] = jnp.zeros_like(state_t)

    nch = tc // c
    nk = GLA_HEADS * GLA_DK
    ri, ci = _iota2((tc, tc), 0), _iota2((tc, tc), 1)
    tril = ((ri // c) == (ci // c)) & (ri >= ci)
    onorm = onorm_ref[...]
    heads = range(GLA_HEADS)

    pre = _mm(sm_ref[...], wgk_ref[...]) + bgk_ref[...]
    gk = (jnp.minimum(pre, 0.0) - jnp.log(1.0 + jnp.exp(-jnp.abs(pre)))) * (1.0 / GLA_GATE_NORMALIZER)
    b = _mm01(tril.astype(F32), gk)
    b3 = b.reshape(nch, c, nk)
    at_row = lambda r: jnp.broadcast_to(b3[:, r:r + 1, :], (nch, c, nk)).reshape(tc, nk)
    bref, b_last = at_row(c // 2), at_row(c - 1)
    q = qk_ref[:, 0:nk] * (GLA_DK ** -0.5)
    k = qk_ref[:, nk:2 * nk]
    q_in = q * jnp.exp(b - bref)
    k_in = k * jnp.exp(bref - b)
    qg = q * jnp.exp(b)
    kd = k * jnp.exp(b_last - b)
    dec = jnp.exp(b_last)
    hs = lambda x, h: x[:, h * GLA_DK:(h + 1) * GLA_DK]
    vs = [v_ref[:, h * GLA_DV:(h + 1) * GLA_DV] for h in heads]
    a_intra = [jnp.where(tril, _mm_nt(hs(q_in, h), hs(k_in, h)), 0.0) for h in heads]
    o_intra = [_mm(a_intra[h], vs[h]) for h in heads]
    contrib = [[_mm_tn(vs[h][ck * c:(ck + 1) * c], hs(kd, h)[ck * c:(ck + 1) * c]) for ck in range(nch)]
               for h in heads]
    for h in heads:
        st = state_t[h]
        o_inter = []
        for ck in range(nch):
            o_inter.append(_mm_nt(hs(qg, h)[ck * c:(ck + 1) * c], st))
            st = st * hs(dec, h)[ck * c:ck * c + 1, :] + contrib[h][ck]
        state_t[h] = st
        o = o_intra[h] + jnp.concatenate(o_inter, axis=0)
        o = o * lax.rsqrt(jnp.mean(o * o, axis=-1, keepdims=True) + NORM_EPS) * onorm
        y_ref[:, h * GLA_DV:(h + 1) * GLA_DV] = o * _silu(z_ref[:, h * GLA_DV:(h + 1) * GLA_DV])


def _gla(u_all, bsz, s_len, w_gk, b_gk, onorm, *, tc=256):
    tc = min(tc, s_len)
    ns = s_len // tc
    sm_blk = _U_LAYOUT['small'][0] // 128
    w_pad = jnp.zeros((128, GLA_HEADS * GLA_DK), F32).at[SM_GK:SM_GK + GLA_LOWRANK].set(w_gk.astype(F32))
    row = lambda b, s: b * ns + s
    return pl.pallas_call(
        functools.partial(_gla_kernel, tc=tc),
        grid=(bsz, ns),
        in_specs=[
            pl.BlockSpec((tc, 512), lambda b, s: (row(b, s), _U_LAYOUT['gla_q'][0] // 512)),
            pl.BlockSpec((tc, 512), lambda b, s: (row(b, s), _U_LAYOUT['gla_v'][0] // 512)),
            pl.BlockSpec((tc, 512), lambda b, s: (row(b, s), _U_LAYOUT['gla_z'][0] // 512)),
            pl.BlockSpec((tc, 128), lambda b, s: (row(b, s), sm_blk)),
            pl.BlockSpec((128, 256), lambda b, s: (0, 0)),
            pl.BlockSpec((1, 256), lambda b, s: (0, 0)),
            pl.BlockSpec((1, GLA_DV), lambda b, s: (0, 0)),
        ],
        out_specs=pl.BlockSpec((tc, 512), lambda b, s: (row(b, s), 0)),
        out_shape=jax.ShapeDtypeStruct((bsz * s_len, 512), F32),
        scratch_shapes=[pltpu.VMEM((GLA_HEADS, GLA_DV, GLA_DK), F32)],
        compiler_params=pltpu.CompilerParams(
            dimension_semantics=("arbitrary", "arbitrary"), vmem_limit_bytes=_VMEM_LIMIT),
        name="gla",
    )(u_all, u_all, u_all, u_all, w_pad, b_gk.reshape(1, -1).astype(F32), onorm.reshape(1, GLA_DV).astype(F32))


def _ssd_kernel(x_ref, bc_ref, z_ref, sm_ref, cw_ref, cb_ref, alog_ref, dtb_ref, dskip_ref, onorm_ref,
                y_ref, xbuf, state, *, tc):
    c = CHUNK
    hg = SSD_HEADS // SSD_GROUPS

    @pl.when(pl.program_id(1) == 0)
    def _():
        state[...] = jnp.zeros_like(state)
        xbuf[0:8, :] = jnp.zeros((8, xbuf.shape[1]), F32)

    xbuf[8:8 + tc, 0:512] = x_ref[...]
    xbuf[8:8 + tc, 512:1024] = bc_ref[...]

    nch = tc // c
    inner = SSD_HEADS * SSD_HEADDIM
    ri, ci = _iota2((tc, tc), 0), _iota2((tc, tc), 1)
    tril = ((ri // c) == (ci // c)) & (ri >= ci)
    neg_a = -jnp.exp(alog_ref[...])
    cbias = cb_ref[...]
    heads = range(SSD_HEADS)

    dt_all = _softplus(sm_ref[...] + dtb_ref[...])
    acs_all = _mm01(tril.astype(F32), dt_all * neg_a)
    acs_last_all = jnp.broadcast_to(acs_all.reshape(nch, c, 128)[:, c - 1:c, :], (nch, c, 128)).reshape(tc, 128)
    spread_m = (_iota2((128, inner), 0) == SM_DT + _iota2((128, inner), 1) // SSD_HEADDIM).astype(F32)
    spread = lambda cols: _mm10(cols, spread_m)
    xs = _causal_conv_silu(xbuf, 0, 0, inner, cw_ref, cbias[:, 0:inner], rows=tc)
    xdt = xs * spread(dt_all)
    xdec = xdt * spread(jnp.exp(acs_last_all - acs_all))
    bms, cms, cbs = [], [], []
    for g in range(SSD_GROUPS):
        lo = inner + g * SSD_STATE
        bms.append(_causal_conv_silu(xbuf, 0, lo, SSD_STATE, cw_ref, cbias[:, lo:lo + SSD_STATE], rows=tc))
        lo = inner + SSD_GROUPS * SSD_STATE + g * SSD_STATE
        cms.append(_causal_conv_silu(xbuf, 0, lo, SSD_STATE, cw_ref, cbias[:, lo:lo + SSD_STATE], rows=tc))
        cbs.append(_mm_nt(cms[g], bms[g]))
    hl = lambda x, hh: x[:, hh * SSD_HEADDIM:(hh + 1) * SSD_HEADDIM]
    lmats = []
    for hh in heads:
        acs_b = jnp.broadcast_to(acs_all[:, SM_DT + hh:SM_DT + hh + 1], (tc, tc))
        lmats.append(jnp.exp(jnp.where(tril, acs_b - acs_b.T, -jnp.inf)))
    y_diag = [_mm(cbs[hh // hg] * lmats[hh], hl(xdt, hh)) for hh in heads]
    contrib = [[_mm_tn(hl(xdec, hh)[ck * c:(ck + 1) * c], bms[hh // hg][ck * c:(ck + 1) * c]) for ck in range(nch)]
               for hh in heads]
    cdec = jnp.exp(acs_last_all)
    y_off = []
    for hh in heads:
        st = state[hh]
        parts = []
        for ck in range(nch):
            parts.append(_mm_nt(cms[hh // hg][ck * c:(ck + 1) * c], st))
            st = st * cdec[ck * c:ck * c + 1, SM_DT + hh:SM_DT + hh + 1] + contrib[hh][ck]
        state[hh] = st
        y_off.append(jnp.concatenate(parts, axis=0))
    y = (jnp.concatenate(y_diag, axis=1) + jnp.concatenate(y_off, axis=1) * spread(jnp.exp(acs_all))
         + xs * dskip_ref[...])
    y = y * _silu(z_ref[...])
    y_ref[...] = y * lax.rsqrt(jnp.mean(y * y, axis=-1, keepdims=True) + NORM_EPS) * onorm_ref[...]
    xbuf[0:8, 0:512] = x_ref[tc - 8:tc, :]
    xbuf[0:8, 512:1024] = bc_ref[tc - 8:tc, :]


def _ssd(u_all, bsz, s_len, conv_w, conv_b, a_log, dt_bias, d_skip, onorm, *, tc=256):
    tc = min(tc, s_len)
    ns = s_len // tc
    sm_blk = _U_LAYOUT['small'][0] // 128
    row = lambda b, s: b * ns + s
    const = lambda b, s: (0, 0)
    return pl.pallas_call(
        functools.partial(_ssd_kernel, tc=tc),
        grid=(bsz, ns),
        in_specs=[
            pl.BlockSpec((tc, 512), lambda b, s: (row(b, s), _U_LAYOUT['ssd_x'][0] // 512)),
            pl.BlockSpec((tc, 512), lambda b, s: (row(b, s), _U_LAYOUT['ssd_b'][0] // 512)),
            pl.BlockSpec((tc, 512), lambda b, s: (row(b, s), _U_LAYOUT['ssd_z'][0] // 512)),
            pl.BlockSpec((tc, 128), lambda b, s: (row(b, s), sm_blk)),
            pl.BlockSpec((CONV_K, 1024), const),
            pl.BlockSpec((1, 1024), const),
            pl.BlockSpec((1, 128), const),
            pl.BlockSpec((1, 128), const),
            pl.BlockSpec((1, 512), const),
            pl.BlockSpec((1, 512), const),
        ],
        out_specs=pl.BlockSpec((tc, 512), lambda b, s: (row(b, s), 0)),
        out_shape=jax.ShapeDtypeStruct((bsz * s_len, 512), F32),
        scratch_shapes=[pltpu.VMEM((tc + 8, 1024), F32), pltpu.VMEM((SSD_HEADS, SSD_HEADDIM, SSD_STATE), F32)],
        compiler_params=pltpu.CompilerParams(
            dimension_semantics=("arbitrary", "arbitrary"), vmem_limit_bytes=_VMEM_LIMIT),
        name="ssd",
    )(u_all, u_all, u_all, u_all, conv_w.astype(F32), conv_b.reshape(1, -1).astype(F32),
      _small_row(a_log, SM_DT), _small_row(dt_bias, SM_DT),
      jnp.repeat(d_skip.astype(F32), SSD_HEADDIM).reshape(1, 512), onorm.reshape(1, 512).astype(F32))


def _compress_kernel(kc_ref, vc_ref, pos_ref, w1_ref, w2_ref, o_ref, *, nsub):
    for kv, ref in enumerate((kc_ref, vc_ref)):
        e = jnp.zeros((nsub, 128), F32)
        f = jnp.zeros((nsub, 128), F32)
        for r in range(CMP_STRIDE):
            x = ref[pl.ds(r, nsub, stride=CMP_STRIDE), :]
            e = e + _mm(x + pos_ref[kv, r], w1_ref[kv, r])
            f = f + _mm(x + pos_ref[kv, CMP_STRIDE + r], w1_ref[kv, CMP_STRIDE + r])
        pre = e + pltpu.roll(f, nsub - 1, 0)
        o_ref[kv, 0] = _mm(_silu(pre), w2_ref[kv])


def _per_group(w):
    eye = jnp.eye(NSA_KV_HEADS, dtype=w.dtype)
    out = jnp.einsum('gh,...ij->...gihj', eye, w)
    return out.reshape(w.shape[:-2] + (NSA_KV_HEADS * w.shape[-2], NSA_KV_HEADS * w.shape[-1]))


def _compress(u_all, bsz, s_len, pos, w1, w2):
    hd = NSA_HEADDIM
    nsub = s_len // CMP_STRIDE
    w1_bd = _per_group(w1.reshape(2, CMP_BLOCK, hd, hd)).astype(_MXU)
    pos_bd = jnp.tile(pos.astype(F32), (1, 1, NSA_KV_HEADS)).reshape(2, CMP_BLOCK, 1, NSA_KV_HEADS * hd)
    const = lambda n: (lambda b: (0,) * n)
    return pl.pallas_call(
        functools.partial(_compress_kernel, nsub=nsub),
        grid=(bsz,),
        in_specs=[
            pl.BlockSpec((s_len, 128), lambda b: (b, _U_LAYOUT['nsa_kc'][0] // 128)),
            pl.BlockSpec((s_len, 128), lambda b: (b, _U_LAYOUT['nsa_vc'][0] // 128)),
            pl.BlockSpec(pos_bd.shape, const(4)),
            pl.BlockSpec(w1_bd.shape, const(4)),
            pl.BlockSpec((2, 128, 128), const(3)),
        ],
        out_specs=pl.BlockSpec((2, 1, nsub, 128), lambda b: (0, b, 0, 0)),
        out_shape=jax.ShapeDtypeStruct((2, bsz, nsub, 128), F32),
        compiler_params=pltpu.CompilerParams(dimension_semantics=("arbitrary",), vmem_limit_bytes=_VMEM_LIMIT),
        name="nsa_compress",
    )(u_all, u_all, pos_bd, w1_bd, _per_group(w2).astype(_MXU))


def _kvprep_kernel(ks_ref, vs_ref, kw_ref, vw_ref, fs_ref, fw_ref, ksel_ref, vsel_ref, kwin_ref, vwin_ref):
    hd = NSA_HEADDIM
    low = _iota2(ks_ref.shape, 1) < hd
    for k_ref, f_ref, o_ref in ((ks_ref, fs_ref, ksel_ref), (kw_ref, fw_ref, kwin_ref)):
        k = k_ref[...]
        f = f_ref[...]
        o_ref[0] = jnp.where(low, k, f).astype(o_ref.dtype)
        o_ref[1] = jnp.where(low, pltpu.roll(k, hd, 1), f).astype(o_ref.dtype)
    for v_ref, o_ref in ((vs_ref, vsel_ref), (vw_ref, vwin_ref)):
        vt = v_ref[...].T
        o_ref[0, 0] = vt[:hd].astype(o_ref.dtype)
        o_ref[1, 0] = vt[hd:].astype(o_ref.dtype)


def _kvprep(u_all, bsz, s_len, feat_sel, feat_win, *, tk):
    hd, ngrp = NSA_HEADDIM, NSA_KV_HEADS
    nt = s_len // tk
    ucol = lambda name: pl.BlockSpec((tk, 128), lambda b, j: (b * nt + j, _U_LAYOUT[name][0] // 128))
    fspec = pl.BlockSpec((tk, 128), lambda b, j: (j, 0))
    kspec = pl.BlockSpec((ngrp, tk, 128), lambda b, j: (b, j, 0))
    vspec = pl.BlockSpec((ngrp, 1, hd, tk), lambda b, j: (b, j, 0, 0))
    kshape = jax.ShapeDtypeStruct((bsz * ngrp, s_len, 128), _MXU)
    vshape = jax.ShapeDtypeStruct((bsz * ngrp, nt, hd, tk), _MXU)
    return pl.pallas_call(
        _kvprep_kernel,
        grid=(bsz, nt),
        in_specs=[ucol('nsa_ks'), ucol('nsa_vs'), ucol('nsa_kw'), ucol('nsa_vw'), fspec, fspec],
        out_specs=[kspec, vspec, kspec, vspec],
        out_shape=[kshape, vshape, kshape, vshape],
        compiler_params=pltpu.CompilerParams(
            dimension_semantics=("arbitrary", "arbitrary"), vmem_limit_bytes=_VMEM_LIMIT),
        name="nsa_kv_prep",
    )(u_all, u_all, u_all, u_all, feat_sel, feat_win)


def _cmpattn_kernel(slopes_ref, q_ref, kc_ref, vc_ref, ovt_ref, oc_ref, qsel_ref, used_ref, *, tq, nsel, nc, topk):
    hd, ngrp = NSA_HEADDIM, NSA_KV_HEADS
    hg = NSA_HEADS // ngrp
    q0 = pl.program_id(1) * tq
    nsub = kc_ref.shape[2]
    t = q0 + _iota2((tq, nsub), 0)
    ccol = _iota2((tq, nsub), 1)
    cpos = ccol * CMP_STRIDE + (CMP_BLOCK - 1)
    valid = (cpos <= t) & (ccol < nc)
    dist = (t - cpos).astype(F32)
    lane = _iota2((tq, 2 * hd), 1)
    kcs = [kc_ref[0, 0][:, g * hd:(g + 1) * hd] for g in range(ngrp)]
    vcs = [vc_ref[0, 0][:, g * hd:(g + 1) * hd] for g in range(ngrp)]
    heads = [(g, h) for g in range(ngrp) for h in range(hg)]

    def q_slab(idx):
        slab = q_ref[:, (idx // 2) * 2 * hd:(idx // 2 + 1) * 2 * hd] * (hd ** -0.5)
        return slab if idx % 2 == 0 else pltpu.roll(slab, hd, 1)

    slabs = [q_slab(g * hg + h) for g, h in heads]
    ss = [_mm_nt(slabs[i][:, :hd], kcs[g]) - slopes_ref[g * hg + h] * dist for i, (g, h) in enumerate(heads)]
    ss = [jnp.where(valid, s, -jnp.inf) for s in ss]
    ms = [jnp.max(s, axis=-1, keepdims=True) for s in ss]
    es = [jnp.exp(s - jnp.where(m > -jnp.inf, m, 0.0)) for s, m in zip(ss, ms)]
    ps = [e / jnp.maximum(jnp.sum(e, axis=-1, keepdims=True), 1e-30) for e in es]
    for i, (g, h) in enumerate(heads):
        oc_ref[g, :, h * hd:(h + 1) * hd] = _mm(ps[i], vcs[g])
    jj = _iota2((nsel, tq), 0)
    cur = (q0 + _iota2((nsel, tq), 1)) // SEL_BLOCK
    forced = (jj == 0) | (jj == cur) | (jj == cur - 1)
    for g in range(ngrp):
        psum = ps[g * hg]
        for h in range(1, hg):
            psum = psum + ps[g * hg + h]
        imp = _mm01(ovt_ref[...], psum, b_contract=1)
        imp = jnp.where(jj <= cur, jnp.where(forced, FORCED_SCORE, imp), -1.0)
        rank = jnp.zeros((nsel, tq), jnp.int32)
        for j2 in range(nsel):
            row = imp[j2:j2 + 1, :]
            before = (row > imp) | ((row == imp) & (jj > j2))
            rank = rank + before.astype(jnp.int32)
        picked = (rank < topk) & (imp >= 0.0)
        npicked = picked.astype(F32) if g == 0 else npicked + picked.astype(F32)
        bias_t = jnp.where(picked, 0.0, _UNSELECTED)
        pieces = [jnp.zeros((hd, tq), F32), bias_t]
        if nsel < hd:
            pieces.append(jnp.zeros((hd - nsel, tq), F32))
        bias = jnp.concatenate(pieces, axis=0).T
        for h in range(hg):
            feat = bias + jnp.where(lane == hd, slopes_ref[g * hg + h], 0.0)
            qsel_ref[g, h] = jnp.where(lane < hd, slabs[g * hg + h], feat).astype(qsel_ref.dtype)
    nkt = used_ref.shape[2]
    in_tile = (_iota2((nkt, nsel), 0) == _iota2((nkt, nsel), 1) // (nsel // nkt)).astype(F32)
    per_tile = jnp.sum(_mm(in_tile, npicked), axis=-1, keepdims=True)
    used_ref[0, 0] = jnp.broadcast_to(per_tile, (nkt, 128))


def _cmpattn(slopes, u_all, cmp, ovt, bsz, s_len, *, nc, tq, tk):
    hd, ngrp = NSA_HEADDIM, NSA_KV_HEADS
    hg = NSA_HEADS // ngrp
    nkt = s_len // tk
    nq = s_len // tq
    nsel = s_len // SEL_BLOCK
    nsub = cmp.shape[2]
    qw = NSA_HEADS * hd
    return pl.pallas_call(
        functools.partial(_cmpattn_kernel, tq=tq, nsel=nsel, nc=nc, topk=min(SEL_TOPK, nsel)),
        grid=(bsz, nq),
        in_specs=[
            pl.BlockSpec(memory_space=pltpu.SMEM),
            pl.BlockSpec((tq, qw), lambda i, j: (i * nq + j, _U_LAYOUT['nsa_q'][0] // qw)),
            pl.BlockSpec((1, 1, nsub, ngrp * hd), lambda i, j: (0, i, 0, 0)),
            pl.BlockSpec((1, 1, nsub, ngrp * hd), lambda i, j: (1, i, 0, 0)),
            pl.BlockSpec((nsel, nsub), lambda i, j: (0, 0)),
        ],
        out_specs=[
            pl.BlockSpec((ngrp, tq, hg * hd), lambda i, j: (i, j, 0)),
            pl.BlockSpec((ngrp, hg, tq, 2 * hd), lambda i, j: (i, 0, j, 0)),
            pl.BlockSpec((1, 1, nkt, 128), lambda i, j: (i, j, 0, 0)),
        ],
        out_shape=[
            jax.ShapeDtypeStruct((bsz * ngrp, s_len, hg * hd), F32),
            jax.ShapeDtypeStruct((bsz * ngrp, hg, s_len, 2 * hd), _MXU),
            jax.ShapeDtypeStruct((bsz, nq, nkt, 128), F32),
        ],
        compiler_params=pltpu.CompilerParams(
            dimension_semantics=("arbitrary", "arbitrary"), vmem_limit_bytes=_VMEM_LIMIT),
        name="nsa_cmp_attn",
    )(slopes, u_all, cmp, cmp, ovt)


def _flash_tiles(q_alls, k_ref, vt_ref, slope_rows, q0, spans, *, tq, tk, hg, window, needed=None):
    rel = _iota2((tk, tq), 1) - _iota2((tk, tq), 0)
    ngrp = len(q_alls)

    def step(masked, kt, carries):
        k0 = pl.multiple_of(kt * tk, tk)
        if masked:
            d = rel + (q0 - k0)
            ok = d >= 0
            if window is not None:
                ok = ok & (d < window)
            mb = jnp.where(ok, 0.0, _MASKED)
            mb = jnp.concatenate([mb] * hg, axis=1)
        grps = range(ngrp)
        sts = [_mm_nt(k_ref[g, pl.ds(k0, tk), :], q_alls[g]) for g in grps]
        if masked:
            sts = [st + mb for st in sts]
        offs = [slope_rows[g] * k0.astype(F32) for g in grps]
        m_news = [jnp.maximum(carries[g][0], jnp.max(sts[g], axis=0, keepdims=True) + offs[g]) for g in grps]
        ps = [jnp.exp(sts[g] - (m_news[g] - offs[g])) for g in grps]
        alphas = [jnp.exp(carries[g][0] - m_news[g]) for g in grps]
        ls = [alphas[g] * carries[g][1] + jnp.sum(ps[g], axis=0, keepdims=True) for g in grps]
        accs = [alphas[g] * carries[g][2] + _mm(vt_ref[g, kt], ps[g]) for g in grps]
        return tuple((m_news[g], ls[g], accs[g]) for g in grps)

    def pair_step(i, carries):
        kts = (a + 2 * i, a + 2 * i + 1)
        k0s = [pl.multiple_of(kt * tk, tk) for kt in kts]
        grps = range(ngrp)
        sts = [[_mm_nt(k_ref[g, pl.ds(k0, tk), :], q_alls[g]) for g in grps] for k0 in k0s]
        offs = [[slope_rows[g] * k0.astype(F32) for g in grps] for k0 in k0s]
        m_news = [jnp.maximum(carries[g][0],
                              jnp.maximum(jnp.max(sts[0][g], axis=0, keepdims=True) + offs[0][g],
                                          jnp.max(sts[1][g], axis=0, keepdims=True) + offs[1][g])) for g in grps]
        ps = [[jnp.exp(sts[t][g] - (m_news[g] - offs[t][g])) for g in grps] for t in range(2)]
        alphas = [jnp.exp(carries[g][0] - m_news[g]) for g in grps]
        ls = [alphas[g] * carries[g][1] + jnp.sum(ps[0][g], axis=0, keepdims=True)
              + jnp.sum(ps[1][g], axis=0, keepdims=True) for g in grps]
        accs = [alphas[g] * carries[g][2] + _mm(vt_ref[g, kts[0]], ps[0][g]) + _mm(vt_ref[g, kts[1]], ps[1][g])
                for g in grps]
        return tuple((m_news[g], ls[g], accs[g]) for g in grps)

    rows = hg * tq
    init = (jnp.full((1, rows), _MASKED, F32), jnp.zeros((1, rows), F32), jnp.zeros((NSA_HEADDIM, rows), F32))
    carries = (init,) * ngrp
    lo, a, b, hi = spans
    plain_step = functools.partial(step, False)
    if needed is None:
        pair_body, single_body = pair_step, plain_step
    else:
        def pair_body(i, c):
            want = needed(a + 2 * i) | needed(a + 2 * i + 1)
            return lax.cond(want, lambda cc: pair_step(i, cc), lambda cc: cc, c)

        def single_body(kt, c):
            return lax.cond(needed(kt), lambda cc: plain_step(kt, cc), lambda cc: cc, c)

    carries = lax.fori_loop(lo, a, functools.partial(step, True), carries)
    npairs = (b - a) // 2
    carries = lax.fori_loop(0, npairs, pair_body, carries)
    carries = lax.fori_loop(a + 2 * npairs, b, single_body, carries)
    carries = lax.fori_loop(b, hi, functools.partial(step, True), carries)
    return [acc / l for _, l, acc in carries]


def _slope_row(slopes_ref, g, hg, tq):
    return jnp.concatenate([jnp.full((1, tq), slopes_ref[g * hg + h], F32) for h in range(hg)], axis=1)


def _heads_to_lanes(ot, hg, tq):
    pairs = []
    for h in range(0, hg, 2):
        two = jnp.concatenate([ot[:, h * tq:(h + 1) * tq], ot[:, (h + 1) * tq:(h + 2) * tq]], axis=0)
        pairs.append(two.T)
    return jnp.concatenate(pairs, axis=1)


def _selattn_kernel(slopes_ref, used_ref, q_ref, k_ref, vt_ref, o_ref, *, tq, tk, nkt):
    ngrp, hg = NSA_KV_HEADS, NSA_HEADS // NSA_KV_HEADS
    q0 = pl.program_id(1) * tq
    full_hi = (q0 + 1) // tk
    hi = (q0 + tq - 1) // tk + 1
    q_alls = [q_ref[g].reshape(hg * tq, q_ref.shape[-1]) for g in range(ngrp)]
    slope_rows = [_slope_row(slopes_ref, g, hg, tq) for g in range(ngrp)]
    base = (pl.program_id(0) * pl.num_programs(1) + pl.program_id(1)) * nkt
    ots = _flash_tiles(q_alls, k_ref, vt_ref, slope_rows, q0, (0, 0, full_hi, hi),
                       tq=tq, tk=tk, hg=hg, window=None, needed=lambda kt: used_ref[base + kt] > 0)
    for g in range(ngrp):
        o_ref[g] = _heads_to_lanes(ots[g], hg, tq)


def _selattn(slopes, used, qsel, ksel, vsel_t, *, tq, tk):
    bg, hg, s_len, wid = qsel.shape
    hd, ngrp = NSA_HEADDIM, NSA_KV_HEADS
    return pl.pallas_call(
        functools.partial(_selattn_kernel, tq=tq, tk=tk, nkt=s_len // tk),
        grid=(bg // ngrp, s_len // tq),
        in_specs=[
            pl.BlockSpec(memory_space=pltpu.SMEM),
            pl.BlockSpec(memory_space=pltpu.SMEM),
            pl.BlockSpec((ngrp, hg, tq, wid), lambda i, j: (i, 0, j, 0)),
            pl.BlockSpec((ngrp, s_len, wid), lambda i, j: (i, 0, 0)),
            pl.BlockSpec((ngrp, s_len // tk, hd, tk), lambda i, j: (i, 0, 0, 0)),
        ],
        out_specs=pl.BlockSpec((ngrp, tq, hg * hd), lambda i, j: (i, j, 0)),
        out_shape=jax.ShapeDtypeStruct((bg, s_len, hg * hd), F32),
        compiler_params=pltpu.CompilerParams(
            dimension_semantics=("arbitrary", "arbitrary"), vmem_limit_bytes=_VMEM_LIMIT),
        name="nsa_sel_attn",
    )(slopes, used, qsel, ksel, vsel_t)


def _winattn_kernel(slopes_ref, q_ref, k_ref, vt_ref, oc_ref, os_ref, gate_ref, gexp_ref, z0_ref, z1_ref, y_ref,
                    *, tq, tk):
    ngrp, hg = NSA_KV_HEADS, NSA_HEADS // NSA_KV_HEADS
    gw = hg * NSA_HEADDIM
    q0 = pl.program_id(1) * tq
    lo = jnp.maximum(q0 - (WINDOW - 1), 0) // tk
    full_hi = (q0 + 1) // tk
    full_lo = jnp.minimum(jnp.maximum(q0 + tq - WINDOW + tk - 1, 0) // tk, full_hi)
    hi = (q0 + tq - 1) // tk + 1
    q_alls = [q_ref[g].reshape(hg * tq, q_ref.shape[-1]) for g in range(ngrp)]
    slope_rows = [_slope_row(slopes_ref, g, hg, tq) for g in range(ngrp)]
    ots = _flash_tiles(q_alls, k_ref, vt_ref, slope_rows, q0, (lo, full_lo, full_hi, hi),
                       tq=tq, tk=tk, hg=hg, window=WINDOW)
    gates = jax.nn.sigmoid(gate_ref[...])
    for g, z_ref in enumerate((z0_ref, z1_ref)):
        o_w = _heads_to_lanes(ots[g], hg, tq)
        spread = lambda c: _mm10(gates, gexp_ref[g, c])
        o = spread(0) * oc_ref[g] + spread(1) * os_ref[g] + spread(2) * o_w
        y_ref[:, g * gw:(g + 1) * gw] = o * _silu(z_ref[...])


def _winattn(slopes, q_pad, kwin, vwin_t, o_c, o_s, u_all, *, tq, tk):
    bg, hg, s_len, wid = q_pad.shape
    hd, ngrp = NSA_HEADDIM, NSA_KV_HEADS
    nq = s_len // tq
    gw = hg * hd
    zblk = _U_LAYOUT['nsa_z'][0] // gw
    head = jnp.arange(ngrp)[:, None, None, None] * hg + jnp.arange(gw)[None, None, None, :] // hd
    gexp = (jnp.arange(128)[None, None, :, None] == SM_GATE + 3 * head + jnp.arange(3)[None, :, None, None]).astype(F32)
    return pl.pallas_call(
        functools.partial(_winattn_kernel, tq=tq, tk=tk),
        grid=(bg // ngrp, nq),
        in_specs=[
            pl.BlockSpec(memory_space=pltpu.SMEM),
            pl.BlockSpec((ngrp, hg, tq, wid), lambda i, j: (i, 0, j, 0)),
            pl.BlockSpec((ngrp, s_len, wid), lambda i, j: (i, 0, 0)),
            pl.BlockSpec((ngrp, s_len // tk, hd, tk), lambda i, j: (i, 0, 0, 0)),
            pl.BlockSpec((ngrp, tq, gw), lambda i, j: (i, j, 0)),
            pl.BlockSpec((ngrp, tq, gw), lambda i, j: (i, j, 0)),
            pl.BlockSpec((tq, 128), lambda i, j: (i * nq + j, _U_LAYOUT['small'][0] // 128)),
            pl.BlockSpec((ngrp, 3, 128, gw), lambda i, j: (0, 0, 0, 0)),
            pl.BlockSpec((tq, gw), lambda i, j: (i * nq + j, zblk)),
            pl.BlockSpec((tq, gw), lambda i, j: (i * nq + j, zblk + 1)),
        ],
        out_specs=pl.BlockSpec((tq, ngrp * gw), lambda i, j: (i * nq + j, 0)),
        out_shape=jax.ShapeDtypeStruct((bg // ngrp * s_len, ngrp * gw), F32),
        compiler_params=pltpu.CompilerParams(
            dimension_semantics=("arbitrary", "arbitrary"), vmem_limit_bytes=_VMEM_LIMIT),
        name="nsa_win_attn",
    )(slopes, q_pad, kwin, vwin_t, o_c, o_s, u_all, gexp, u_all, u_all)


def _nsa(u_all, bsz, s_len, cmp_pos_k, cmp_pos_v, w_ck1, w_ck2, w_cv1, w_cv2):
    hd = NSA_HEADDIM
    tq, tk = min(_NSA_TQ, s_len), min(_NSA_TK, s_len)
    slopes = 2.0 ** (-8.0 * jnp.arange(1, NSA_HEADS + 1, dtype=F32) / NSA_HEADS)
    nsel = s_len // SEL_BLOCK
    pos = jnp.arange(s_len)
    tile_off = (pos % tk).astype(F32)[:, None]
    onehot = (pos[:, None] // SEL_BLOCK == jnp.arange(1, hd)[None, :]).astype(F32)
    feat_sel = jnp.concatenate([jnp.zeros((s_len, hd), F32), tile_off, onehot], axis=-1)
    feat_win = jnp.concatenate([jnp.zeros((s_len, hd), F32), tile_off, jnp.zeros((s_len, hd - 1), F32)], axis=-1)
    ksel, vsel_t, kwin, vwin_t = _kvprep(u_all, bsz, s_len, feat_sel, feat_win, tk=tk)
    nsub = s_len // CMP_STRIDE
    nc = nsub - CMP_BLOCK // CMP_STRIDE + 1
    cmp = _compress(u_all, bsz, s_len, jnp.stack([cmp_pos_k, cmp_pos_v]), jnp.stack([w_ck1, w_cv1]),
                    jnp.stack([w_ck2, w_cv2]))
    cmp_start = jnp.arange(nsub) * CMP_STRIDE
    sel_start = jnp.arange(nsel) * SEL_BLOCK
    ovt = ((cmp_start[None, :] <= sel_start[:, None] + SEL_BLOCK - 1)
           & (cmp_start[None, :] + CMP_BLOCK - 1 >= sel_start[:, None])
           & (jnp.arange(nsub)[None, :] < nc)).astype(F32)
    o_c, qsel, picked_per_tile = _cmpattn(slopes, u_all, cmp, ovt, bsz, s_len, nc=nc, tq=tq, tk=tk)
    used = (picked_per_tile[..., 0] > 0).astype(jnp.int32).reshape(-1)
    o_s = _selattn(slopes, used, qsel, ksel, vsel_t, tq=tq, tk=tk)
    return _winattn(slopes, qsel, kwin, vwin_t, o_c, o_s, u_all, tq=tq, tk=tk)


def _merge_kernel(y0_ref, y1_ref, y2_ref, y3_ref, g0_ref, g1_ref, g2_ref, g3_ref, wbr_ref, wout_ref, gain_ref,
                  x_ref, o_ref):
    merged = None
    for n, (y_ref, g_ref) in enumerate(((y0_ref, g0_ref), (y1_ref, g1_ref), (y2_ref, g2_ref), (y3_ref, g3_ref))):
        term = jax.nn.sigmoid(g_ref[...]) * jnp.dot(y_ref[...].astype(_MXU), wbr_ref[n], preferred_element_type=F32)
        merged = term if merged is None else merged + term
    out = jnp.dot(merged.astype(_MXU), wout_ref[...], preferred_element_type=F32)
    out = out * lax.rsqrt(jnp.mean(out * out, axis=-1, keepdims=True) + NORM_EPS) * gain_ref[...]
    o_ref[...] = x_ref[...] + out


def _merge(ys, u_all, w_br, w_out, gain, x2d, *, tm=512):
    t = x2d.shape[0]
    tm = min(tm, t)
    gblk = _U_LAYOUT['merge_gate'][0] // D_MODEL
    yspec = pl.BlockSpec((tm, BR_WIDTH), lambda i: (i, 0))
    gspec = lambda n: pl.BlockSpec((tm, D_MODEL), lambda i: (i, gblk + n))
    return pl.pallas_call(
        _merge_kernel,
        grid=(t // tm,),
        in_specs=[yspec] * 4 + [gspec(n) for n in range(4)] + [
            pl.BlockSpec((N_BRANCH, BR_WIDTH, D_MODEL), lambda i: (0, 0, 0)),
            pl.BlockSpec((D_MODEL, D_MODEL), lambda i: (0, 0)),
            pl.BlockSpec((1, D_MODEL), lambda i: (0, 0)),
            pl.BlockSpec((tm, D_MODEL), lambda i: (i, 0)),
        ],
        out_specs=pl.BlockSpec((tm, D_MODEL), lambda i: (i, 0)),
        out_shape=jax.ShapeDtypeStruct((t, D_MODEL), F32),
        compiler_params=pltpu.CompilerParams(dimension_semantics=("arbitrary",), vmem_limit_bytes=_VMEM_LIMIT),
        name="merge",
    )(*ys, u_all, u_all, u_all, u_all, w_br.astype(_MXU), w_out.astype(_MXU), gain.reshape(1, D_MODEL).astype(F32), x2d)


def _layer(x2d, bsz, s_len, p):
    u_all = _inproj(x2d, p['norm_pre'].astype(F32), p['w_all'])
    ys = (
        _gdn(u_all, bsz, s_len, p['conv_a'], p['a_log_a'], p['dt_bias_a'], p['onorm_a']),
        _gla(u_all, bsz, s_len, p['w_gk'], p['b_gk'], p['onorm_b']),
        _ssd(u_all, bsz, s_len, p['conv_c'], p['conv_bias_c'], p['a_log_c'], p['dt_bias_c'], p['d_skip_c'], p['onorm_c']),
        _nsa(u_all, bsz, s_len, p['cmp_pos_k'], p['cmp_pos_v'], p['w_ck1'], p['w_ck2'], p['w_cv1'], p['w_cv2']),
    )
    return _merge(ys, u_all, p['w_br'], p['w_out'], p['norm_post'], x2d)


def kernel(x, norm_pre, norm_post, w_in, conv_a, a_log_a, dt_bias_a, onorm_a, w_gk, b_gk, onorm_b, conv_c, conv_bias_c, a_log_c, dt_bias_c, d_skip_c, onorm_c, cmp_pos_k, cmp_pos_v, w_ck1, w_ck2, w_cv1, w_cv2, w_br, w_out):
    bsz, s_len, _ = x.shape
    params = dict(norm_pre=norm_pre, norm_post=norm_post, w_all=_layout_weights(w_in).astype(_MXU),
                  conv_a=conv_a, a_log_a=a_log_a,
                  dt_bias_a=dt_bias_a, onorm_a=onorm_a, w_gk=w_gk, b_gk=b_gk, onorm_b=onorm_b, conv_c=conv_c,
                  conv_bias_c=conv_bias_c, a_log_c=a_log_c, dt_bias_c=dt_bias_c, d_skip_c=d_skip_c,
                  onorm_c=onorm_c, cmp_pos_k=cmp_pos_k, cmp_pos_v=cmp_pos_v, w_ck1=w_ck1, w_ck2=w_ck2,
                  w_cv1=w_cv1, w_cv2=w_cv2, w_br=w_br, w_out=w_out)
    x2d = x.reshape(bsz * s_len, D_MODEL)
    for l in range(w_in.shape[0]):
        x2d = _layer(x2d, bsz, s_len, {k: v[l] for k, v in params.items()})
    return x2d.reshape(bsz, s_len, D_MODEL)
```
